```python
import jax
import jax.numpy as jnp
from jax import lax
import numpy as np

D_MODEL = 1024
BATCH = 1
SEQ = 16384
DEPTH = 2

GRID_W = 64
CTX_LEN = 256
N_GROUPS = 4
GROUP_W = D_MODEL // N_GROUPS
D_MIX = N_GROUPS * GROUP_W
CONV_W = 3
CHUNK = 128
CMLP_HEADS = 4
CMLP_HD = GROUP_W // CMLP_HEADS
ATT_HEADS = 4
ATT_KV_HEADS = 2
ATT_GROUP = ATT_HEADS // ATT_KV_HEADS
ATT_HD = GROUP_W // ATT_HEADS
Q_BLOCK = 128
ROPE_THETA = 10000.0
ML_HEADS = 4
ML_HD = GROUP_W // ML_HEADS
ML_CHUNK = 128
ML_FGATE_BIAS = 3.0
N_EXPERTS = 16
EXPERT_FF = 2 * D_MODEL
CAPACITY_FACTOR = 2
MOD_PARTS = 6
EPS = 1e-6

IN_LAYOUT = (
    ('conv_x', GROUP_W), ('conv_b', GROUP_W), ('conv_c', GROUP_W),
    ('cmlp_u', GROUP_W), ('cmlp_v', GROUP_W),
    ('att_q', ATT_HEADS * ATT_HD), ('att_k', ATT_KV_HEADS * ATT_HD), ('att_v', ATT_KV_HEADS * ATT_HD),
    ('ml_q', GROUP_W), ('ml_k', GROUP_W), ('ml_v', GROUP_W), ('ml_o', GROUP_W),
    ('ml_i', 2 * ML_HEADS), ('ml_f', 2 * ML_HEADS),
)
D_IN = sum(s for _, s in IN_LAYOUT)

kernel_name = 'hybrid_diffusion_prefix_block'


def rmsnorm(x, g):
    xf = x.astype(jnp.float32)
    y = xf * lax.rsqrt(jnp.mean(xf * xf, axis=-1, keepdims=True) + EPS)
    return (y * g.astype(jnp.float32)).astype(x.dtype)


def modulate(h, shift, scale):
    return h * (1 + scale) + shift


def split_in(z):
    sizes = [s for _, s in IN_LAYOUT]
    parts = jnp.split(z, np.cumsum(sizes)[:-1].tolist(), axis=-1)
    return {name: part for (name, _), part in zip(IN_LAYOUT, parts)}


def axial_rope_tables(rows):
    pos = jnp.arange(rows * GRID_W)
    row, col = pos // GRID_W, pos % GRID_W
    axis_dim = ATT_HD // 2
    inv_freq = ROPE_THETA ** (-jnp.arange(0, axis_dim, 2, dtype=jnp.float32) / axis_dim)

    def axis_angles(p):
        a = p.astype(jnp.float32)[:, None] * inv_freq[None, :]
        return jnp.concatenate([a, a], axis=-1)

    ang = jnp.concatenate([axis_angles(row), axis_angles(col)], axis=-1)
    return jnp.cos(ang), jnp.sin(ang)


def rotate_half_axial(x):
    axis_dim = ATT_HD // 2
    quarter = axis_dim // 2

    def rh(s):
        return jnp.concatenate([-s[..., quarter:], s[..., :quarter]], axis=-1)

    return jnp.concatenate([rh(x[..., :axis_dim]), rh(x[..., axis_dim:])], axis=-1)


def apply_rope(x, cos, sin):
    xf = x.astype(jnp.float32)
    return (xf * cos + rotate_half_axial(xf) * sin).astype(x.dtype)


def short_conv_mixer(xin, gate_b, gate_c, conv_w):
    y = gate_c * xin
    y = lax.conv_general_dilated(
        y, conv_w[:, None, :], window_strides=(1,), padding=[(CONV_W // 2, CONV_W // 2)],
        dimension_numbers=('NWC', 'WIO', 'NWC'), feature_group_count=GROUP_W)
    return gate_b * y


def chunk_token_mlp(u, v, norm_g, ws, bs):
    b, l, _ = v.shape
    vn = rmsnorm(v, norm_g).reshape(b, l // CHUNK, CHUNK, CMLP_HEADS, CMLP_HD)
    mixed = jnp.einsum('hts,bcshd->bcthd', ws, vn) + bs.T[None, None, :, :, None]
    return u * mixed.reshape(b, l, GROUP_W)


def attn_qkv(pq, pk, pv, q_norm_g, k_norm_g):
    b, l, _ = pq.shape
    q = rmsnorm(pq.reshape(b, l, ATT_HEADS, ATT_HD), q_norm_g).transpose(0, 2, 1, 3)
    k = rmsnorm(pk.reshape(b, l, ATT_KV_HEADS, ATT_HD), k_norm_g).transpose(0, 2, 1, 3)
    v = pv.reshape(b, l, ATT_KV_HEADS, ATT_HD).transpose(0, 2, 1, 3)
    return q, k, v


def gqa_attend(q, k, v):
    s = jnp.einsum('bkgqd,bksd->bkgqs', q, k).astype(jnp.float32) * (ATT_HD ** -0.5)
    p = jax.nn.softmax(s, axis=-1).astype(v.dtype)
    return jnp.einsum('bkgqs,bksd->bkgqd', p, v)


def context_attention(q, k, v):
    b, _, l, _ = q.shape
    o = gqa_attend(q.reshape(b, ATT_KV_HEADS, ATT_GROUP, l, ATT_HD), k, v)
    return o.transpose(0, 3, 1, 2, 4).reshape(b, l, ATT_HEADS * ATT_HD)


def latent_attention(q, k, v, k_ctx, v_ctx):
    b, _, n, _ = q.shape
    k_all = jnp.concatenate([k_ctx, k], axis=2)
    v_all = jnp.concatenate([v_ctx, v], axis=2)
    nb = n // Q_BLOCK
    qb = q.reshape(b, ATT_KV_HEADS, ATT_GROUP, nb, Q_BLOCK, ATT_HD).transpose(3, 0, 1, 2, 4, 5)
    out = lax.map(lambda qblk: gqa_attend(qblk, k_all, v_all), qb)
    return out.transpose(1, 0, 4, 2, 3, 5).reshape(b, n, ATT_HEADS * ATT_HD)


def mlstm_heads(t):
    b, l, _ = t.shape
    return t.reshape(b, l, ML_HEADS, ML_HD).transpose(0, 2, 1, 3).astype(jnp.float32)


def mlstm_inputs(pq, pk, pv, pi, pf, igate_b, fgate_b):
    q = mlstm_heads(pq)
    k = mlstm_heads(pk) * (ML_HD ** -0.5)
    v = mlstm_heads(pv)
    li = (pi.astype(jnp.float32) + igate_b.astype(jnp.float32)).transpose(0, 2, 1)
    lf = jax.nn.log_sigmoid(pf.astype(jnp.float32) + fgate_b.astype(jnp.float32)).transpose(0, 2, 1)
    fwd = (li[:, :ML_HEADS], lf[:, :ML_HEADS])
    bwd = (li[:, ML_HEADS:], lf[:, ML_HEADS:])
    return q, k, v, fwd, bwd


def zero_state(b):
    return (jnp.zeros((b, ML_HEADS, ML_HD, ML_HD), jnp.float32),
            jnp.zeros((b, ML_HEADS, ML_HD), jnp.float32),
            jnp.zeros((b, ML_HEADS), jnp.float32))


def flip_seq(t):
    return jnp.flip(t, axis=2)


def to_chunks(t):
    b, h, l = t.shape[:3]
    return t.reshape(b, h, l // ML_CHUNK, ML_CHUNK, *t.shape[3:])


def mlstm_chunk_states(k, v, li, lf, state0):
    b_cum = jnp.cumsum(lf, axis=-1)
    b_last = b_cum[..., -1]
    a = b_last[..., None] - b_cum + li
    m_loc = jnp.max(a, axis=-1)
    w = jnp.exp(a - m_loc[..., None])
    c_loc = jnp.einsum('bhcs,bhcsv,bhcsk->bhcvk', w, v, k)
    n_loc = jnp.einsum('bhcs,bhcsk->bhck', w, k)

    def step(carry, inp):
        c_prev, n_prev, m_prev = carry
        bl, ml, cl, nl = inp
        m_new = jnp.maximum(bl + m_prev, ml)
        d_old = jnp.exp(bl + m_prev - m_new)
        d_loc = jnp.exp(ml - m_new)
        c_new = d_old[..., None, None] * c_prev + d_loc[..., None, None] * cl
        n_new = d_old[..., None] * n_prev + d_loc[..., None] * nl
        return (c_new, n_new, m_new), (c_prev, n_prev, m_prev)

    xs = tuple(jnp.moveaxis(t, 2, 0) for t in (b_last, m_loc, c_loc, n_loc))
    final, prev = lax.scan(step, state0, xs)
    prev = tuple(jnp.moveaxis(t, 0, 2) for t in prev)
    return b_cum, prev, final


def mlstm_chunk_outputs(q, k, v, li, b_cum, prev):
    c_prev, n_prev, m_prev = prev
    L = q.shape[-2]
    lower = jnp.tril(jnp.ones((L, L), dtype=bool))
    dmat = b_cum[..., :, None] - b_cum[..., None, :] + li[..., None, :]
    dmat = jnp.where(lower, dmat, -jnp.inf)
    inter = b_cum + m_prev[..., None]
    m_t = jnp.maximum(inter, jnp.max(dmat, axis=-1))
    s = jnp.einsum('bhctd,bhcsd->bhcts', q, k) * jnp.exp(dmat - m_t[..., None])
    w_inter = jnp.exp(inter - m_t)
    num = jnp.einsum('bhcts,bhcsd->bhctd', s, v) + w_inter[..., None] * jnp.einsum('bhcvk,bhctk->bhctv', c_prev, q)
    den = jnp.sum(s, axis=-1) + w_inter * jnp.einsum('bhck,bhctk->bhct', n_prev, q)
    return num / jnp.maximum(jnp.abs(den), jnp.exp(-m_t))[..., None]


def mlstm_scan(q, k, v, li, lf, state0):
    qc, kc, vc, lic, lfc = (to_chunks(t) for t in (q, k, v, li, lf))
    b_cum, prev, final = mlstm_chunk_states(kc, vc, lic, lfc, state0)
    h = mlstm_chunk_outputs(qc, kc, vc, lic, b_cum, prev)
    return h.reshape(q.shape), final


def mlstm_final_state(k, v, li, lf, state0):
    return mlstm_chunk_states(to_chunks(k), to_chunks(v), to_chunks(li), to_chunks(lf), state0)[2]


def mlstm_merge(h_f, h_b, po, norm_g):
    b, _, l, _ = h_f.shape
    hs = h_f + h_b
    hs = hs * lax.rsqrt(jnp.mean(hs * hs, axis=-1, keepdims=True) + EPS)
    hs = hs.transpose(0, 2, 1, 3).reshape(b, l, GROUP_W) * norm_g.astype(jnp.float32)
    return (jax.nn.sigmoid(po.astype(jnp.float32)) * hs).astype(po.dtype)


def expert_choice_ffn(h, router_w, w1, w3, w2):
    b, n, _ = h.shape
    cap = CAPACITY_FACTOR * n // N_EXPERTS
    aff = jax.nn.softmax(jnp.einsum('bnd,de->bne', h, router_w).astype(jnp.float32), axis=-1)
    gate, idx = lax.top_k(jnp.swapaxes(aff, 1, 2), cap)
    bidx = jnp.arange(b)[:, None, None]
    xs = h[bidx, idx]
    hid = jax.nn.silu(jnp.einsum('becd,edf->becf', xs, w1)) * jnp.einsum('becd,edf->becf', xs, w3)
    ys = jnp.einsum('becf,efd->becd', hid, w2) * gate[..., None].astype(h.dtype)
    return jnp.zeros_like(h).at[bidx, idx].add(ys)


def local_mixers(p, conv_w, cmlp_norm_g, cmlp_ws, cmlp_bs):
    conv_out = short_conv_mixer(p['conv_x'], p['conv_b'], p['conv_c'], conv_w)
    cmlp_out = chunk_token_mlp(p['cmlp_u'], p['cmlp_v'], cmlp_norm_g, cmlp_ws, cmlp_bs)
    return conv_out, cmlp_out


def layer_forward(x, xc, c, c_ctx, cos, sin, ada_w, ada_b, norm1_g, w_in, conv_w, cmlp_norm_g, cmlp_ws,
                  cmlp_bs, q_norm_g, k_norm_g, ml_igate_b, ml_fgate_b, ml_norm_g, w_out, norm2_g, router_w,
                  exp_w1, exp_w3, exp_w2, update_ctx):
    bsz = x.shape[0]
    mod = (jax.nn.silu(c) @ ada_w + ada_b)[:, None, :]
    mod_c = (jax.nn.silu(c_ctx) @ ada_w + ada_b)[None, None, :]
    shift1, scale1, gate1, shift2, scale2, gate2 = jnp.split(mod, MOD_PARTS, axis=-1)
    cshift1, cscale1, cgate1, cshift2, cscale2, cgate2 = jnp.split(mod_c, MOD_PARTS, axis=-1)

    pc = split_in(modulate(rmsnorm(xc, norm1_g), cshift1, cscale1) @ w_in)
    qc, kc, vc = attn_qkv(pc['att_q'], pc['att_k'], pc['att_v'], q_norm_g, k_norm_g)
    mq_c, mk_c, mv_c, (li_cf, lf_cf), (li_cb, lf_cb) = mlstm_inputs(
        pc['ml_q'], pc['ml_k'], pc['ml_v'], pc['ml_i'], pc['ml_f'], ml_igate_b, ml_fgate_b)
    state0 = zero_state(bsz)
    if update_ctx:
        hcf, st_f = mlstm_scan(mq_c, mk_c, mv_c, li_cf, lf_cf, state0)
        hcb, st_b = mlstm_scan(flip_seq(mq_c), flip_seq(mk_c), flip_seq(mv_c), flip_seq(li_cb), flip_seq(lf_cb), state0)
        conv_c, cmlp_c = local_mixers(pc, conv_w, cmlp_norm_g, cmlp_ws, cmlp_bs)
        att_c = context_attention(qc, kc, vc)
        ml_c = mlstm_merge(hcf, flip_seq(hcb), pc['ml_o'], ml_norm_g)
        mix_c = jnp.concatenate([conv_c, cmlp_c, att_c.astype(xc.dtype), ml_c.astype(xc.dtype)], axis=-1) @ w_out
        xc_new = xc + cgate1 * mix_c
        hc2 = modulate(rmsnorm(xc_new, norm2_g), cshift2, cscale2)
        xc_new = xc_new + cgate2 * expert_choice_ffn(hc2, router_w, exp_w1, exp_w3, exp_w2)
    else:
        st_f = mlstm_final_state(mk_c, mv_c, li_cf, lf_cf, state0)
        st_b = mlstm_final_state(flip_seq(mk_c), flip_seq(mv_c), flip_seq(li_cb), flip_seq(lf_cb), state0)
        xc_new = xc

    p = split_in(modulate(rmsnorm(x, norm1_g), shift1, scale1) @ w_in)
    conv_x, cmlp_x = local_mixers(p, conv_w, cmlp_norm_g, cmlp_ws, cmlp_bs)
    q, k, v = attn_qkv(p['att_q'], p['att_k'], p['att_v'], q_norm_g, k_norm_g)
    q = apply_rope(q, cos, sin)
    k = apply_rope(k, cos, sin)
    att_x = latent_attention(q, k, v, kc, vc)
    mq, mk, mv, (li_f, lf_f), (li_b, lf_b) = mlstm_inputs(
        p['ml_q'], p['ml_k'], p['ml_v'], p['ml_i'], p['ml_f'], ml_igate_b, ml_fgate_b)
    hf, _ = mlstm_scan(mq, mk, mv, li_f, lf_f, st_f)
    hb, _ = mlstm_scan(flip_seq(mq), flip_seq(mk), flip_seq(mv), flip_seq(li_b), flip_seq(lf_b), st_b)
    ml_x = mlstm_merge(hf, flip_seq(hb), p['ml_o'], ml_norm_g)
    mix = jnp.concatenate([conv_x, cmlp_x, att_x.astype(x.dtype), ml_x.astype(x.dtype)], axis=-1) @ w_out
    x = x + gate1 * mix
    h2 = modulate(rmsnorm(x, norm2_g), shift2, scale2)
    x = x + gate2 * expert_choice_ffn(h2, router_w, exp_w1, exp_w3, exp_w2)
    return x, xc_new


def setup_inputs(seed: int = 0) -> dict:
    key = jax.random.key(seed)
    ks = jax.random.split(key, 24)
    f32 = jnp.float32
    L = DEPTH

    def nrm(k, shape, scale):
        return jax.random.normal(k, shape, f32) * scale

    def gain(k, shape):
        return 1.0 + 0.01 * jax.random.normal(k, shape, f32)

    return {
        'x': nrm(ks[0], (BATCH, SEQ, D_MODEL), 1.0),
        'c': nrm(ks[1], (BATCH, D_MODEL), 1.0),
        'ctx': nrm(ks[2], (BATCH, CTX_LEN, D_MODEL), 1.0),
        'c_ctx': nrm(ks[3], (D_MODEL,), 1.0),
        'ada_w': nrm(ks[4], (L, D_MODEL, MOD_PARTS * D_MODEL), 0.3 * D_MODEL ** -0.5),
        'ada_b': nrm(ks[5], (L, MOD_PARTS * D_MODEL), 0.01),
        'norm1_g': gain(ks[6], (L, D_MODEL)),
        'w_in': nrm(ks[7], (L, D_MODEL, D_IN), D_MODEL ** -0.5),
        'conv_w': nrm(ks[8], (L, CONV_W, GROUP_W), CONV_W ** -0.5),
        'cmlp_norm_g': gain(ks[9], (L, GROUP_W)),
        'cmlp_ws': nrm(ks[10], (L, CMLP_HEADS, CHUNK, CHUNK), CHUNK ** -0.5),
        'cmlp_bs': gain(ks[11], (L, CMLP_HEADS, CHUNK)),
        'q_norm_g': gain(ks[12], (L, ATT_HD)),
        'k_norm_g': gain(ks[13], (L, ATT_HD)),
        'ml_igate_b': nrm(ks[14], (L, 2 * ML_HEADS), 0.1),
        'ml_fgate_b': ML_FGATE_BIAS + nrm(ks[15], (L, 2 * ML_HEADS), 0.1),
        'ml_norm_g': gain(ks[16], (L, GROUP_W)),
        'w_out': nrm(ks[17], (L, D_MIX, D_MODEL), D_MIX ** -0.5),
        'norm2_g': gain(ks[18], (L, D_MODEL)),
        'router_w': nrm(ks[19], (L, D_MODEL, N_EXPERTS), D_MODEL ** -0.5),
        'exp_w1': nrm(ks[20], (L, N_EXPERTS, D_MODEL, EXPERT_FF), D_MODEL ** -0.5),
        'exp_w3': nrm(ks[21], (L, N_EXPERTS, D_MODEL, EXPERT_FF), D_MODEL ** -0.5),
        'exp_w2': nrm(ks[22], (L, N_EXPERTS, EXPERT_FF, D_MODEL), EXPERT_FF ** -0.5),
        'final_norm_g': gain(ks[23], (D_MODEL,)),
    }


def reference(x, c, ctx, c_ctx, ada_w, ada_b, norm1_g, w_in, conv_w, cmlp_norm_g, cmlp_ws, cmlp_bs,
              q_norm_g, k_norm_g, ml_igate_b, ml_fgate_b, ml_norm_g, w_out, norm2_g, router_w,
              exp_w1, exp_w3, exp_w2, final_norm_g):
    rows = x.shape[1] // GRID_W
    cos, sin = axial_rope_tables(rows)
    xc = ctx
    for l in range(DEPTH):
        x, xc = layer_forward(
            x, xc, c, c_ctx, cos, sin, ada_w[l], ada_b[l], norm1_g[l], w_in[l], conv_w[l], cmlp_norm_g[l],
            cmlp_ws[l], cmlp_bs[l], q_norm_g[l], k_norm_g[l], ml_igate_b[l], ml_fgate_b[l], ml_norm_g[l],
            w_out[l], norm2_g[l], router_w[l], exp_w1[l], exp_w3[l], exp_w2[l], update_ctx=(l < DEPTH - 1))
    return rmsnorm(x, final_norm_g)
```

```python
import functools

import jax
import jax.numpy as jnp
from jax import lax
from jax.experimental import pallas as pl
from jax.experimental.pallas import tpu as pltpu

F32 = jnp.float32
BF16 = jnp.bfloat16
HIGHEST = lax.Precision.HIGHEST

EPS = 1e-6
GRID_W = 64
ROPE_THETA = 10000.0
CAPACITY_FACTOR = 2
GROUP_W = 256
HEAD_DIM = 64
CHUNK = 128
N_IN_BLOCKS = 23

LANES = 128
SUBLANES = 8
VMEM_LIMIT = 56 * 1024 * 1024

ROUTE_BLOCKS = 128
ROUTE_ITERS = 48
GATHER_WIN = 136
COMBINE_WIN = 256

NT_DIMS = (((1,), (1,)), ((), ()))


def _cparams(sem, vmem=None):
    return pltpu.CompilerParams(dimension_semantics=sem, vmem_limit_bytes=vmem or VMEM_LIMIT)


def _sigmoid(x):
    return 1.0 / (1.0 + jnp.exp(-x))


def _iota(shape, dim):
    return lax.broadcasted_iota(jnp.int32, shape, dim)


def _mod_kernel(cc_ref, w_ref, b_ref, o_ref):
    cc = cc_ref[...]
    s = cc * _sigmoid(cc)
    w = w_ref[0]
    b = b_ref[0]
    o_ref[0, 0:1, :] = jnp.sum(w * s[:, 0:1], axis=0, keepdims=True) + b
    o_ref[0, 1:2, :] = jnp.sum(w * s[:, 1:2], axis=0, keepdims=True) + b


def _modulation(cc, ada_w, ada_b):
    depth, d, m = ada_w.shape
    tn = 1536
    return pl.pallas_call(
        _mod_kernel,
        grid=(depth, m // tn),
        in_specs=[pl.BlockSpec((d, 2), lambda l, j: (0, 0)),
                  pl.BlockSpec((1, d, tn), lambda l, j: (l, 0, j)),
                  pl.BlockSpec((1, 1, tn), lambda l, j: (l, 0, j))],
        out_specs=pl.BlockSpec((1, 2, tn), lambda l, j: (l, 0, j)),
        out_shape=jax.ShapeDtypeStruct((depth, 2, m), F32),
        compiler_params=_cparams(("arbitrary", "arbitrary")),
        name="modulation",
    )(cc, ada_w, ada_b.reshape(depth, 1, m))


def _inproj_kernel(*refs, has_res):
    if has_res:
        x_ref, y_ref, g2_ref, ng_ref, sh_ref, sc_ref, w_ref, z_ref, xr_ref = refs
        x = x_ref[...] + g2_ref[...] * y_ref[...]
        xr_ref[...] = x
    else:
        x_ref, ng_ref, sh_ref, sc_ref, w_ref, z_ref = refs
        x = x_ref[...]
    ms = jnp.mean(x * x, axis=-1, keepdims=True)
    h = x * lax.rsqrt(ms + EPS) * ng_ref[...]
    h = h * (1.0 + sc_ref[...]) + sh_ref[...]
    z_ref[...] = jnp.dot(h.astype(BF16), w_ref[...], preferred_element_type=F32)


def _inproj(x, res, norm_g, shift, scale, w_in):
    n, d = x.shape
    dz = w_in.shape[1]
    tn = min(n, 512)
    row = pl.BlockSpec((tn, d), lambda i: (i, 0))
    vec = pl.BlockSpec((1, d), lambda i: (0, 0))
    wsp = pl.BlockSpec((d, dz), lambda i: (0, 0))
    zsp = pl.BlockSpec((tn, dz), lambda i: (i, 0))
    if res is None:
        z = pl.pallas_call(
            functools.partial(_inproj_kernel, has_res=False),
            grid=(n // tn,),
            in_specs=[row, vec, vec, vec, wsp],
            out_specs=zsp,
            out_shape=jax.ShapeDtypeStruct((n, dz), F32),
            compiler_params=_cparams(("arbitrary",)),
            name="inproj",
        )(x, norm_g, shift, scale, w_in)
        return z, x
    y, g2 = res
    z, xr = pl.pallas_call(
        functools.partial(_inproj_kernel, has_res=True),
        grid=(n // tn,),
        in_specs=[row, row, vec, vec, vec, vec, wsp],
        out_specs=[zsp, row],
        out_shape=[jax.ShapeDtypeStruct((n, dz), F32), jax.ShapeDtypeStruct((n, d), F32)],
        compiler_params=_cparams(("arbitrary",)),
        name="inproj_res",
    )(x, y, g2, norm_g, shift, scale, w_in)
    return z, xr


def _local_kernel(cx_ref, cb_ref, cc_ref, u_ref, v_ref, px_ref, pc_ref, nx_ref, nc_ref,
                  cw_ref, g_ref, ws_ref, bs_ref, conv_ref, cmlp_ref, *, tn):
    i = pl.program_id(0)
    last = pl.num_programs(0) - 1
    cx = cc_ref[...] * cx_ref[...]
    prev = (pc_ref[...] * px_ref[...])[SUBLANES - 1:SUBLANES, :]
    nxt = (nc_ref[...] * nx_ref[...])[0:1, :]
    prev = jnp.where(i == 0, 0.0, prev)
    nxt = jnp.where(i == last, 0.0, nxt)
    row = _iota((tn, 1), 0)
    up = jnp.where(row == 0, prev, pltpu.roll(cx, 1, 0))
    dn = jnp.where(row == tn - 1, nxt, pltpu.roll(cx, tn - 1, 0))
    w = cw_ref[...]
    conv_ref[...] = cb_ref[...] * (w[0:1] * up + w[1:2] * cx + w[2:3] * dn)

    v = v_ref[...]
    vn = v * lax.rsqrt(jnp.mean(v * v, axis=-1, keepdims=True) + EPS) * g_ref[...]
    head = _iota((1, GROUP_W), 1) >> 6
    for c in range(tn // CHUNK):
        sl = slice(c * CHUNK, (c + 1) * CHUNK)
        vc = vn[sl]
        acc = bs_ref[...]
        for h in range(GROUP_W // HEAD_DIM):
            vm = jnp.where(head == h, vc, 0.0).astype(BF16)
            acc = acc + jnp.dot(ws_ref[h], vm, preferred_element_type=F32)
        cmlp_ref[sl, :] = u_ref[sl, :] * acc


def _local_mixers(z, conv_w, cmlp_g, cmlp_ws, cmlp_bias):
    n = z.shape[0]
    tn = min(n, 256)
    tb = tn // SUBLANES
    nb8 = n // SUBLANES

    def col(j):
        return pl.BlockSpec((tn, GROUP_W), lambda i: (i, j))

    def prev(j):
        return pl.BlockSpec((SUBLANES, GROUP_W), lambda i: (jnp.maximum(i * tb - 1, 0), j))

    def nxt(j):
        return pl.BlockSpec((SUBLANES, GROUP_W), lambda i: (jnp.minimum((i + 1) * tb, nb8 - 1), j))

    out = pl.BlockSpec((tn, GROUP_W), lambda i: (i, 0))
    return pl.pallas_call(
        functools.partial(_local_kernel, tn=tn),
        grid=(n // tn,),
        in_specs=[col(0), col(1), col(2), col(3), col(4), prev(0), prev(2), nxt(0), nxt(2),
                  pl.BlockSpec((3, GROUP_W), lambda i: (0, 0)),
                  pl.BlockSpec((1, GROUP_W), lambda i: (0, 0)),
                  pl.BlockSpec((4, CHUNK, CHUNK), lambda i: (0, 0, 0)),
                  pl.BlockSpec((CHUNK, GROUP_W), lambda i: (0, 0))],
        out_specs=[out, out],
        out_shape=[jax.ShapeDtypeStruct((n, GROUP_W), F32)] * 2,
        compiler_params=_cparams(("arbitrary",)),
        name="local_mixers",
    )(z, z, z, z, z, z, z, z, z, conv_w, cmlp_g, cmlp_ws, cmlp_bias)


def _attprep_kernel(*refs, rope):
    if rope:
        q_ref, kv_ref, qg_ref, kg_ref, cos_ref, sin_ref, qo_ref, kt_ref, vd_ref = refs
    else:
        q_ref, kv_ref, qg_ref, kg_ref, qo_ref, kt_ref, vd_ref = refs
    q = q_ref[...]
    kv = kv_ref[...]
    k = kv[:, :LANES]
    v = kv[:, LANES:]
    same_head = (_iota((LANES, LANES), 0) >> 6) == (_iota((LANES, LANES), 1) >> 6)
    bd = jnp.where(same_head, 1.0 / HEAD_DIM, 0.0)
    lane = _iota((1, LANES), 1)
    first_half = (lane & 31) < 16

    def head_norm(x, g):
        ms = jnp.dot(x * x, bd, precision=HIGHEST, preferred_element_type=F32)
        return x * lax.rsqrt(ms + EPS) * g

    def rotary(x):
        if not rope:
            return x
        rot = jnp.where(first_half, -pltpu.roll(x, LANES - 16, 1), pltpu.roll(x, 16, 1))
        return x * cos_ref[...] + rot * sin_ref[...]

    for half in range(2):
        sl = slice(half * LANES, (half + 1) * LANES)
        qh = rotary(head_norm(q[:, sl], qg_ref[...]))
        qo_ref[:, sl] = (qh * (HEAD_DIM ** -0.5)).astype(BF16)
    kn = rotary(head_norm(k, kg_ref[...]))
    kt = kn.T.astype(BF16)
    kt_ref[0:64, :] = kt[0:64]
    kt_ref[64:128, :] = kt[0:64]
    kt_ref[128:192, :] = kt[64:128]
    kt_ref[192:256, :] = kt[64:128]
    vr = pltpu.roll(v, HEAD_DIM, 1)
    low = lane < HEAD_DIM
    vd_ref[:, 0:LANES] = jnp.where(low, v, vr).astype(BF16)
    vd_ref[:, LANES:] = jnp.where(low, vr, v).astype(BF16)


def _attprep(z, q_g, k_g, tables):
    n = z.shape[0]
    tn = min(n, 256)
    rope = tables is not None
    in_specs = [pl.BlockSpec((tn, GROUP_W), lambda i: (i, 5)),
                pl.BlockSpec((tn, GROUP_W), lambda i: (i, 6)),
                pl.BlockSpec((1, LANES), lambda i: (0, 0)),
                pl.BlockSpec((1, LANES), lambda i: (0, 0))]
    args = [z, z, q_g, k_g]
    if rope:
        in_specs += [pl.BlockSpec((tn, LANES), lambda i: (i, 0))] * 2
        args += list(tables)
    return pl.pallas_call(
        functools.partial(_attprep_kernel, rope=rope),
        grid=(n // tn,),
        in_specs=in_specs,
        out_specs=[pl.BlockSpec((tn, GROUP_W), lambda i: (i, 0)),
                   pl.BlockSpec((GROUP_W, tn), lambda i: (0, i)),
                   pl.BlockSpec((tn, GROUP_W), lambda i: (i, 0))],
        out_shape=[jax.ShapeDtypeStruct((n, GROUP_W), BF16),
                   jax.ShapeDtypeStruct((GROUP_W, n), BF16),
                   jax.ShapeDtypeStruct((n, GROUP_W), BF16)],
        compiler_params=_cparams(("arbitrary",)),
        name="attprep_rope" if rope else "attprep",
    )(*args)


def _flash_kernel(q_ref, k_ref, v_ref, o_ref, *, tk, nk):
    q = q_ref[...]
    tq = q.shape[0]
    low = _iota((1, LANES), 1) < HEAD_DIM
    zero = jnp.zeros_like(q)
    qa = jnp.where(low, q, zero)
    qb = jnp.where(low, zero, q)

    def step(j, carry):
        m_a, l_a, m_b, l_b, acc = carry
        start = pl.multiple_of(j * tk, LANES)
        k = k_ref[:, pl.ds(start, tk)]
        v = v_ref[pl.ds(start, tk), :]
        zv = jnp.zeros_like(v)
        s_a = jnp.dot(qa, k, preferred_element_type=F32)
        s_b = jnp.dot(qb, k, preferred_element_type=F32)
        n_a = jnp.maximum(m_a, jnp.max(s_a, axis=1, keepdims=True))
        n_b = jnp.maximum(m_b, jnp.max(s_b, axis=1, keepdims=True))
        al_a = jnp.exp(m_a - n_a)
        al_b = jnp.exp(m_b - n_b)
        p_a = jnp.exp(s_a - n_a)
        p_b = jnp.exp(s_b - n_b)
        l_a = al_a * l_a + jnp.sum(p_a, axis=1, keepdims=True)
        l_b = al_b * l_b + jnp.sum(p_b, axis=1, keepdims=True)
        pv = (jnp.dot(p_a.astype(BF16), jnp.where(low, v, zv), preferred_element_type=F32)
              + jnp.dot(p_b.astype(BF16), jnp.where(low, zv, v), preferred_element_type=F32))
        acc = acc * jnp.where(low, al_a, al_b) + pv
        return n_a, l_a, n_b, l_b, acc

    neg = jnp.full((tq, 1), -jnp.inf, F32)
    zl = jnp.zeros((tq, 1), F32)
    _, l_a, _, l_b, acc = lax.fori_loop(0, nk, step, (neg, zl, neg, zl, jnp.zeros((tq, LANES), F32)))
    o_ref[...] = acc / jnp.where(low, l_a, l_b)


def _kv_tile(nk):
    best = LANES
    for t in range(LANES, 2048 + 1, LANES):
        if nk % t == 0:
            best = t
    return best


def _flash(q, kt, vd):
    n = q.shape[0]
    nk = kt.shape[1]
    tq = min(n, 256)
    tk = _kv_tile(nk)
    return pl.pallas_call(
        functools.partial(_flash_kernel, tk=tk, nk=nk // tk),
        grid=(2, n // tq),
        in_specs=[pl.BlockSpec((tq, LANES), lambda g, i: (i, g)),
                  pl.BlockSpec((LANES, nk), lambda g, i: (g, 0)),
                  pl.BlockSpec((nk, LANES), lambda g, i: (0, g))],
        out_specs=pl.BlockSpec((tq, LANES), lambda g, i: (i, g)),
        out_shape=jax.ShapeDtypeStruct((n, GROUP_W), F32),
        compiler_params=_cparams(("arbitrary", "arbitrary")),
        name="flash_attention",
    )(q, kt, vd)


def _mlstm_kernel(*refs, fwd, merge):
    if merge:
        (q_ref, k_ref, v_ref, zg_ref, gb_ref, c0_ref, n0_ref, m0_ref, hf_ref, po_ref, ng_ref,
         h_ref, c_ref, n_ref, m_ref) = refs
    else:
        (q_ref, k_ref, v_ref, zg_ref, gb_ref, c0_ref, n0_ref, m0_ref,
         h_ref, c_ref, n_ref, m_ref) = refs

    @pl.when(pl.program_id(0) == 0)
    def _():
        c_ref[...] = c0_ref[...]
        n_ref[...] = n0_ref[...]
        m_ref[...] = m0_ref[...]

    off = 0 if fwd else 4
    nh = GROUP_W // HEAD_DIM
    q = q_ref[...]
    ks = k_ref[...] * (HEAD_DIM ** -0.5)
    v = v_ref[...]
    g = zg_ref[...] + gb_ref[...]
    lane = _iota((1, LANES), 1)
    lsig = jnp.minimum(g, 0.0) - jnp.log(1.0 + jnp.exp(-jnp.abs(g)))
    gcol = jnp.where((lane >= 8) & (lane < 16), lsig, g)
    lfcol = pltpu.roll(gcol, LANES - 8, 1)
    rr = _iota((CHUNK, CHUNK), 0)
    cc = _iota((CHUNK, CHUNK), 1)
    sees = (cc <= rr) if fwd else (cc >= rr)
    tri = jnp.where(sees, 1.0, 0.0)
    bc = jnp.dot(tri, lfcol, precision=HIGHEST, preferred_element_type=F32)
    bct = bc.T
    gt = gcol.T
    total = bc[CHUNK - 1:CHUNK, :] if fwd else bc[0:1, :]
    cp = c_ref[...]
    npv = n_ref[0:1, :]
    mp = m_ref[0:1, :]
    head = _iota((1, GROUP_W), 1) >> 6
    qb = q.astype(BF16)
    kb = ks.astype(BF16)
    qc = lax.dot_general(qb, cp.astype(BF16), NT_DIMS, preferred_element_type=F32)
    eh = jnp.where((_iota((GROUP_W, LANES), 0) >> 6) == _iota((GROUP_W, LANES), 1), 1.0, 0.0)
    nq = jnp.dot((q * npv).astype(BF16), eh.astype(BF16), preferred_element_type=F32)

    num = jnp.zeros((CHUNK, GROUP_W), F32)
    wi_x = jnp.zeros((CHUNK, GROUP_W), F32)
    den_x = jnp.ones((CHUNK, GROUP_W), F32)
    for h in range(nh):
        ch = off + h
        bc_t = bc[:, ch:ch + 1]
        dm = jnp.where(sees, bc_t - bct[ch:ch + 1, :] + gt[ch:ch + 1, :], -jnp.inf)
        inter = bc_t + mp[:, ch:ch + 1]
        m_t = jnp.maximum(inter, jnp.max(dm, axis=1, keepdims=True))
        hm = head == h
        qh = jnp.where(hm, q, 0.0).astype(BF16)
        s = lax.dot_general(qh, kb, NT_DIMS, preferred_element_type=F32) * jnp.exp(dm - m_t)
        wi = jnp.exp(inter - m_t)
        vh = jnp.where(hm, v, 0.0).astype(BF16)
        num = num + jnp.dot(s.astype(BF16), vh, preferred_element_type=F32)
        den = jnp.sum(s, axis=1, keepdims=True) + wi * nq[:, h:h + 1]
        dd = jnp.maximum(jnp.abs(den), jnp.exp(-m_t))
        wi_x = jnp.where(hm, wi, wi_x)
        den_x = jnp.where(hm, dd, den_x)
    hout = (num + wi_x * qc) / den_x

    a = total - bc + gcol
    m_loc = jnp.max(a, axis=0, keepdims=True)
    w = jnp.exp(a - m_loc)
    m_new = jnp.maximum(total + mp, m_loc)
    d_old = jnp.exp(total + mp - m_new)
    d_loc = jnp.exp(m_loc - m_new)
    w_x = jnp.zeros((CHUNK, GROUP_W), F32)
    dold_x = jnp.zeros((1, GROUP_W), F32)
    dloc_x = jnp.zeros((1, GROUP_W), F32)
    for h in range(nh):
        ch = off + h
        hm = head == h
        w_x = jnp.where(hm, w[:, ch:ch + 1], w_x)
        dold_x = jnp.where(hm, d_old[:, ch:ch + 1], dold_x)
        dloc_x = jnp.where(hm, d_loc[:, ch:ch + 1], dloc_x)
    c_loc = jnp.dot((w_x * v).T.astype(BF16), kb, preferred_element_type=F32)
    same_head = (_iota((GROUP_W, GROUP_W), 0) >> 6) == (_iota((GROUP_W, GROUP_W), 1) >> 6)
    c_ref[...] = cp * dold_x + jnp.where(same_head, c_loc, 0.0) * dloc_x
    n_new = npv * dold_x + jnp.sum(w_x * ks, axis=0, keepdims=True) * dloc_x
    n_ref[...] = jnp.broadcast_to(n_new, n_ref.shape)
    m_ref[...] = jnp.broadcast_to(m_new, m_ref.shape)

    if merge:
        hs = hf_ref[...] + hout
        bd = jnp.where(same_head, 1.0 / HEAD_DIM, 0.0)
        ms = jnp.dot(hs * hs, bd, precision=HIGHEST, preferred_element_type=F32)
        h_ref[...] = _sigmoid(po_ref[...]) * (hs * lax.rsqrt(ms + EPS) * ng_ref[...])
    else:
        h_ref[...] = hout


def _mlstm(z, gate_b, state, *, fwd, merge=None):
    n = z.shape[0]
    nc = n // CHUNK
    order = (lambda c: c) if fwd else (lambda c: nc - 1 - c)

    def col(j):
        return pl.BlockSpec((CHUNK, GROUP_W), lambda c: (order(c), j))

    def full(shape):
        return pl.BlockSpec(shape, lambda c: (0,) * len(shape))

    st_shapes = [(GROUP_W, GROUP_W), (SUBLANES, GROUP_W), (SUBLANES, LANES)]
    in_specs = [col(7), col(8), col(9),
                pl.BlockSpec((CHUNK, LANES), lambda c: (order(c), N_IN_BLOCKS - 1)),
                full((1, LANES))] + [full(s) for s in st_shapes]
    args = [z, z, z, z, gate_b, *state]
    if merge is not None:
        hf, norm_g = merge
        in_specs += [pl.BlockSpec((CHUNK, GROUP_W), lambda c: (order(c), 0)), col(10), full((1, GROUP_W))]
        args += [hf, z, norm_g]
    outs = pl.pallas_call(
        functools.partial(_mlstm_kernel, fwd=fwd, merge=merge is not None),
        grid=(nc,),
        in_specs=in_specs,
        out_specs=[pl.BlockSpec((CHUNK, GROUP_W), lambda c: (order(c), 0))] + [full(s) for s in st_shapes],
        out_shape=[jax.ShapeDtypeStruct((n, GROUP_W), F32)] + [jax.ShapeDtypeStruct(s, F32) for s in st_shapes],
        compiler_params=_cparams(("arbitrary",)),
        name="mlstm_fwd" if fwd else "mlstm_bwd",
    )(*args)
    return outs[0], tuple(outs[1:])


def _outproj_kernel(cv_ref, cm_ref, at_ref, ml_ref, w_ref, x_ref, g1_ref, ng_ref, sh_ref, sc_ref, rw_ref,
                    xo_ref, h_ref, aff_ref, *, n_exp):
    mix = jnp.zeros(x_ref.shape, F32)
    for j, ref in enumerate((cv_ref, cm_ref, at_ref, ml_ref)):
        mix = mix + jnp.dot(ref[...].astype(BF16), w_ref[j * GROUP_W:(j + 1) * GROUP_W, :],
                            preferred_element_type=F32)
    x = x_ref[...] + g1_ref[...] * mix
    xo_ref[...] = x
    ms = jnp.mean(x * x, axis=-1, keepdims=True)
    h = x * lax.rsqrt(ms + EPS) * ng_ref[...]
    h = h * (1.0 + sc_ref[...]) + sh_ref[...]
    h_ref[...] = h.astype(BF16)
    logits = jnp.dot(h, rw_ref[...], precision=HIGHEST, preferred_element_type=F32)
    lt = logits.T[0:n_exp, :]
    e = jnp.exp(lt - jnp.max(lt, axis=0, keepdims=True))
    aff_ref[...] = e / jnp.sum(e, axis=0, keepdims=True)


def _outproj(mixers, w_out, x, gate1, norm_g, shift, scale, router_pad, n_exp):
    n, d = x.shape
    tn = min(n, 512)
    mix_spec = pl.BlockSpec((tn, GROUP_W), lambda i: (i, 0))
    row = pl.BlockSpec((tn, d), lambda i: (i, 0))
    vec = pl.BlockSpec((1, d), lambda i: (0, 0))
    return pl.pallas_call(
        functools.partial(_outproj_kernel, n_exp=n_exp),
        grid=(n // tn,),
        in_specs=[mix_spec] * 4 + [pl.BlockSpec((d, d), lambda i: (0, 0)), row, vec, vec, vec, vec,
                                   pl.BlockSpec((d, LANES), lambda i: (0, 0))],
        out_specs=[row, row, pl.BlockSpec((n_exp, tn), lambda i: (0, i))],
        out_shape=[jax.ShapeDtypeStruct((n, d), F32), jax.ShapeDtypeStruct((n, d), BF16),
                   jax.ShapeDtypeStruct((n_exp, n), F32)],
        compiler_params=_cparams(("arbitrary",)),
        name="outproj_router",
    )(*mixers, w_out, x, gate1, norm_g, shift, scale, router_pad)


def _route_kernel(aff_ref, rank_ref, rankt_ref, offs_ref, *, cap, iters):
    aff = aff_ref[...]
    n_exp = aff.shape[0]

    def count(mask):
        c = jnp.where(mask, 1.0, 0.0)
        return jnp.sum(jnp.sum(c, axis=1, keepdims=True), axis=2, keepdims=True)

    def bisect(_, carry):
        lo, hi = carry
        mid = 0.5 * (lo + hi)
        ok = count(aff >= mid) >= cap
        return jnp.where(ok, mid, lo), jnp.where(ok, hi, mid)

    _, hi = lax.fori_loop(0, iters, bisect,
                          (jnp.zeros((n_exp, 1, 1), F32), jnp.full((n_exp, 1, 1), 2.0, F32)))
    below = jnp.where(aff < hi, aff, -1.0)
    thr = jnp.max(jnp.max(below, axis=1, keepdims=True), axis=2, keepdims=True)
    gt = aff > thr
    eq = aff == thr
    need = cap - count(gt)

    rr = _iota((LANES, LANES), 0)
    cc = _iota((LANES, LANES), 1)
    upper = jnp.where(rr <= cc, 1.0, 0.0).astype(BF16)
    lower = jnp.where(cc < rr, 1.0, 0.0).astype(BF16)

    def prefix(x01):
        within = jnp.dot(x01.astype(BF16), upper, preferred_element_type=F32)
        offs = jnp.dot(lower, within.astype(BF16), preferred_element_type=F32)[:, LANES - 1:LANES]
        return within - x01 + offs, offs

    offs_all = jnp.zeros((ROUTE_BLOCKS, LANES), F32)
    for e in range(n_exp):
        gt_e = jnp.where(gt[e], 1.0, 0.0)
        eq_e = jnp.where(eq[e], 1.0, 0.0)
        eq_rank, _ = prefix(eq_e)
        sel = jnp.maximum(gt_e, jnp.where(eq_rank < need[e], eq_e, 0.0))
        rk, offs = prefix(sel)
        rk = jnp.where(sel > 0.0, rk, -1.0)
        rank_ref[e] = rk.astype(jnp.int32)
        rankt_ref[e] = rk.T
        offs_all = jnp.where(cc == e, offs, offs_all)
    offs_ref[...] = offs_all.astype(jnp.int32)


def _route(aff_t, cap):
    n_exp, n = aff_t.shape
    assert n % LANES == 0 and n <= ROUTE_BLOCKS * LANES and cap <= n
    a = aff_t.reshape(n_exp, n // LANES, LANES)
    if n < ROUTE_BLOCKS * LANES:
        a = jnp.pad(a, ((0, 0), (0, ROUTE_BLOCKS - n // LANES), (0, 0)), constant_values=-1.0)
    blk = pl.BlockSpec((n_exp, ROUTE_BLOCKS, LANES), lambda i: (0, 0, 0))
    rank, rank_t, offs = pl.pallas_call(
        functools.partial(_route_kernel, cap=cap, iters=ROUTE_ITERS),
        grid=(1,),
        in_specs=[blk],
        out_specs=[blk, blk, pl.BlockSpec((ROUTE_BLOCKS, LANES), lambda i: (0, 0))],
        out_shape=[jax.ShapeDtypeStruct((n_exp, ROUTE_BLOCKS, LANES), jnp.int32),
                   jax.ShapeDtypeStruct((n_exp, ROUTE_BLOCKS, LANES), F32),
                   jax.ShapeDtypeStruct((ROUTE_BLOCKS, LANES), jnp.int32)],
        compiler_params=_cparams(("arbitrary",)),
        name="route",
    )(a)
    offs_flat = offs[:, :n_exp].T.reshape(-1)
    return a, rank, rank_t, offs_flat


def _gather_kernel(offs_ref, h_ref, rank_ref, gate_ref, xs_ref, gs_ref, acc_ref, gacc_ref, *, cap, spb):
    e = pl.program_id(0)
    sb = pl.program_id(1)

    @pl.when(sb == 0)
    def _():
        acc_ref[...] = jnp.zeros(acc_ref.shape, F32)
        gacc_ref[...] = jnp.zeros(gacc_ref.shape, F32)

    rows = _iota((GATHER_WIN, LANES), 0)

    def body(k, carry):
        b = sb * spb + k
        r0 = offs_ref[e * ROUTE_BLOCKS + b]
        r0a = pl.multiple_of((r0 >> 3) << 3, SUBLANES)
        rrow = rank_ref[0, pl.ds(b, 1), :]
        grow = gate_ref[0, pl.ds(b, 1), :]
        hit = (rrow - r0a) == rows
        hblk = h_ref[pl.ds(pl.multiple_of(k * LANES, LANES), LANES), :]
        onehot = jnp.where(hit, 1.0, 0.0).astype(BF16)
        acc_ref[pl.ds(r0a, GATHER_WIN), :] += jnp.dot(onehot, hblk, preferred_element_type=F32)
        gacc_ref[pl.ds(r0a, GATHER_WIN), :] += jnp.sum(jnp.where(hit, grow, 0.0), axis=1, keepdims=True)
        return carry

    lax.fori_loop(0, spb, body, 0)

    @pl.when(sb == pl.num_programs(1) - 1)
    def _():
        xs_ref[0] = acc_ref[0:cap, :].astype(BF16)
        gs_ref[0] = gacc_ref[0:cap, :]


def _gather(offs_flat, h2, rank, gate, cap):
    n, d = h2.shape
    n_exp = rank.shape[0]
    sbt = min(n, 2048)
    spb = sbt // LANES
    blk = pl.BlockSpec((1, ROUTE_BLOCKS, LANES), lambda e, s, offs: (e, 0, 0))
    return pl.pallas_call(
        functools.partial(_gather_kernel, cap=cap, spb=spb),
        grid_spec=pltpu.PrefetchScalarGridSpec(
            num_scalar_prefetch=1,
            grid=(n_exp, n // sbt),
            in_specs=[pl.BlockSpec((sbt, d), lambda e, s, offs: (s, 0)), blk, blk],
            out_specs=[pl.BlockSpec((1, cap, d), lambda e, s, offs: (e, 0, 0)),
                       pl.BlockSpec((1, cap, 1), lambda e, s, offs: (e, 0, 0))],
            scratch_shapes=[pltpu.VMEM((cap + GATHER_WIN, d), F32),
                            pltpu.VMEM((cap + GATHER_WIN, 1), F32)]),
        out_shape=[jax.ShapeDtypeStruct((n_exp, cap, d), BF16),
                   jax.ShapeDtypeStruct((n_exp, cap, 1), F32)],
        compiler_params=_cparams(("arbitrary", "arbitrary")),
        name="moe_gather",
    )(offs_flat, h2, rank, gate)


def _ffn_kernel(xs_ref, w1_ref, w3_ref, w2_ref, gs_ref, o_ref, *, cap, tm):
    f = pl.program_id(1)

    @pl.when(f == 0)
    def _():
        o_ref[...] = jnp.zeros(o_ref.shape, F32)

    w1 = w1_ref[0].astype(BF16)
    w3 = w3_ref[0].astype(BF16)
    w2 = w2_ref[0].astype(BF16)
    for r in range(cap // tm):
        sl = slice(r * tm, (r + 1) * tm)
        x = xs_ref[0, sl, :]
        a = jnp.dot(x, w1, preferred_element_type=F32)
        b = jnp.dot(x, w3, preferred_element_type=F32)
        hid = (a * _sigmoid(a) * b).astype(BF16)
        o_ref[0, sl, :] += jnp.dot(hid, w2, preferred_element_type=F32)

    @pl.when(f == pl.num_programs(1) - 1)
    def _():
        o_ref[0, 0:cap, :] = o_ref[0, 0:cap, :] * gs_ref[0]


def _ffn(xs, gs, w1, w3, w2):
    n_exp, cap, d = xs.shape
    ff = w1.shape[2]
    tf = 512
    tm = min(cap, 512)
    cap_p = cap + COMBINE_WIN
    return pl.pallas_call(
        functools.partial(_ffn_kernel, cap=cap, tm=tm),
        grid=(n_exp, ff // tf),
        in_specs=[pl.BlockSpec((1, cap, d), lambda e, f: (e, 0, 0)),
                  pl.BlockSpec((1, d, tf), lambda e, f: (e, 0, f)),
                  pl.BlockSpec((1, d, tf), lambda e, f: (e, 0, f)),
                  pl.BlockSpec((1, tf, d), lambda e, f: (e, f, 0)),
                  pl.BlockSpec((1, cap, 1), lambda e, f: (e, 0, 0))],
        out_specs=pl.BlockSpec((1, cap_p, d), lambda e, f: (e, 0, 0)),
        out_shape=jax.ShapeDtypeStruct((n_exp, cap_p, d), F32),
        compiler_params=_cparams(("arbitrary", "arbitrary")),
        name="moe_ffn",
    )(xs, w1, w3, w2, gs)


def _combine_kernel(offs_ref, out_ref, rankt_ref, y_ref, *, spb):
    th = pl.program_id(1)
    e = pl.program_id(2)

    @pl.when(e == 0)
    def _():
        y_ref[...] = jnp.zeros(y_ref.shape, F32)

    rt = rankt_ref[0]
    lane_b = _iota((LANES, LANES), 1)
    lane_w = _iota((LANES, COMBINE_WIN), 1).astype(F32)

    def body(k, carry):
        b = th * spb + k
        r0 = offs_ref[e * ROUTE_BLOCKS + b]
        r0a = pl.multiple_of((r0 >> 3) << 3, SUBLANES)
        col = jnp.sum(jnp.where(lane_b == b, rt, 0.0), axis=1, keepdims=True)
        hit = (col - r0a.astype(F32)) == lane_w
        onehot = jnp.where(hit, 1.0, 0.0).astype(BF16)
        win = out_ref[0, pl.ds(r0a, COMBINE_WIN), :]
        hi = win.astype(BF16)
        lo = (win - hi.astype(F32)).astype(BF16)
        rows = pl.ds(pl.multiple_of(k * LANES, LANES), LANES)
        y_ref[rows, :] += (jnp.dot(onehot, hi, preferred_element_type=F32)
                           + jnp.dot(onehot, lo, preferred_element_type=F32))
        return carry

    lax.fori_loop(0, spb, body, 0)


def _combine(offs_flat, out, rank_t, n):
    n_exp, cap_p, d = out.shape
    tht = min(n, 4096)
    dc = 512
    spb = tht // LANES
    return pl.pallas_call(
        functools.partial(_combine_kernel, spb=spb),
        grid_spec=pltpu.PrefetchScalarGridSpec(
            num_scalar_prefetch=1,
            grid=(d // dc, n // tht, n_exp),
            in_specs=[pl.BlockSpec((1, cap_p, dc), lambda c, t, e, offs: (e, 0, c)),
                      pl.BlockSpec((1, ROUTE_BLOCKS, LANES), lambda c, t, e, offs: (e, 0, 0))],
            out_specs=pl.BlockSpec((tht, dc), lambda c, t, e, offs: (t, c))),
        out_shape=jax.ShapeDtypeStruct((n, d), F32),
        compiler_params=_cparams(("arbitrary", "arbitrary", "arbitrary")),
        name="moe_combine",
    )(offs_flat, out, rank_t)


def _moe(h2, aff_t, w1, w3, w2):
    n = h2.shape[0]
    n_exp = aff_t.shape[0]
    cap = CAPACITY_FACTOR * n // n_exp
    gate, rank, rank_t, offs_flat = _route(aff_t, cap)
    xs, gs = _gather(offs_flat, h2, rank, gate, cap)
    out = _ffn(xs, gs, w1, w3, w2)
    return _combine(offs_flat, out, rank_t, n)


def _final_kernel(x_ref, y_ref, g2_ref, ng_ref, o_ref):
    x = x_ref[...] + g2_ref[...] * y_ref[...]
    ms = jnp.mean(x * x, axis=-1, keepdims=True)
    o_ref[...] = x * lax.rsqrt(ms + EPS) * ng_ref[...]


def _final(x, y, g2, norm_g):
    n, d = x.shape
    tn = min(n, 512)
    row = pl.BlockSpec((tn, d), lambda i: (i, 0))
    vec = pl.BlockSpec((1, d), lambda i: (0, 0))
    return pl.pallas_call(
        _final_kernel,
        grid=(n // tn,),
        in_specs=[row, row, vec, vec],
        out_specs=row,
        out_shape=jax.ShapeDtypeStruct((n, d), F32),
        compiler_params=_cparams(("arbitrary",)),
        name="final_norm",
    )(x, y, g2, norm_g)


def _rope_tables(n):
    pos = jnp.arange(n)
    row, colp = pos // GRID_W, pos % GRID_W
    axis_dim = HEAD_DIM // 2
    inv_freq = ROPE_THETA ** (-jnp.arange(0, axis_dim, 2, dtype=F32) / axis_dim)

    def axis_angles(p):
        a = p.astype(F32)[:, None] * inv_freq[None, :]
        return jnp.concatenate([a, a], axis=-1)

    ang = jnp.concatenate([axis_angles(row), axis_angles(colp)], axis=-1)
    ang = jnp.concatenate([ang, ang], axis=-1)
    return jnp.cos(ang), jnp.sin(ang)


def _zero_state():
    return (jnp.zeros((GROUP_W, GROUP_W), F32), jnp.zeros((SUBLANES, GROUP_W), F32),
            jnp.zeros((SUBLANES, LANES), F32))


def kernel(x, c, ctx, c_ctx, ada_w, ada_b, norm1_g, w_in, conv_w, cmlp_norm_g, cmlp_ws, cmlp_bs, q_norm_g, k_norm_g, ml_igate_b, ml_fgate_b, ml_norm_g, w_out, norm2_g, router_w, exp_w1, exp_w3, exp_w2, final_norm_g):
    assert x.shape[0] == 1 and ctx.shape[0] == 1
    depth, d, d_in = w_in.shape
    n_exp = router_w.shape[-1]
    assert d_in <= N_IN_BLOCKS * LANES and d == 4 * GROUP_W
    xs = x[0]
    xc = ctx[0]
    n = xs.shape[0]

    mod = _modulation(jnp.stack([c[0], c_ctx], axis=1), ada_w, ada_b)
    tables = _rope_tables(n)
    row = lambda v: v.reshape(1, -1)

    res_x = None
    res_c = None
    for l in range(depth):
        parts = [[mod[l, s:s + 1, j * d:(j + 1) * d] for j in range(6)] for s in range(2)]
        sh1, sc1, g1, sh2, sc2, g2 = parts[0]
        csh1, csc1, cg1, csh2, csc2, cg2 = parts[1]
        w_in_l = jnp.pad(w_in[l], ((0, 0), (0, N_IN_BLOCKS * LANES - d_in))).astype(BF16)
        w_out_l = w_out[l].astype(BF16)
        cmlp_ws_l = cmlp_ws[l].astype(BF16)
        cmlp_bias = jnp.repeat(cmlp_bs[l].T, HEAD_DIM, axis=1)
        qg = jnp.tile(q_norm_g[l], 2).reshape(1, LANES)
        kg = jnp.tile(k_norm_g[l], 2).reshape(1, LANES)
        gate_b = jnp.pad(jnp.concatenate([ml_igate_b[l], ml_fgate_b[l]]), (0, LANES - 16)).reshape(1, LANES)
        router_pad = jnp.pad(router_w[l], ((0, 0), (0, LANES - n_exp)))
        update_ctx = l < depth - 1

        zc, xc = _inproj(xc, res_c, row(norm1_g[l]), csh1, csc1, w_in_l)
        qc_, ktc, vdc = _attprep(zc, qg, kg, None)
        hcf, st_f = _mlstm(zc, gate_b, _zero_state(), fwd=True)
        ml_c, st_b = _mlstm(zc, gate_b, _zero_state(), fwd=False, merge=(hcf, row(ml_norm_g[l])))
        if update_ctx:
            conv_c, cmlp_c = _local_mixers(zc, conv_w[l], row(cmlp_norm_g[l]), cmlp_ws_l, cmlp_bias)
            att_c = _flash(qc_, ktc, vdc)
            xc, hc2, affc = _outproj((conv_c, cmlp_c, att_c, ml_c), w_out_l, xc, cg1, row(norm2_g[l]),
                                     csh2, csc2, router_pad, n_exp)
            res_c = (_moe(hc2, affc, exp_w1[l], exp_w3[l], exp_w2[l]), cg2)

        z, xs = _inproj(xs, res_x, row(norm1_g[l]), sh1, sc1, w_in_l)
        conv_x, cmlp_x = _local_mixers(z, conv_w[l], row(cmlp_norm_g[l]), cmlp_ws_l, cmlp_bias)
        q_, kt, vd = _attprep(z, qg, kg, tables)
        att_x = _flash(q_, jnp.concatenate([ktc, kt], axis=1), jnp.concatenate([vdc, vd], axis=0))
        hf, _ = _mlstm(z, gate_b, st_f, fwd=True)
        ml_x, _ = _mlstm(z, gate_b, st_b, fwd=False, merge=(hf, row(ml_norm_g[l])))
        xs, h2, aff = _outproj((conv_x, cmlp_x, att_x, ml_x), w_out_l, xs, g1, row(norm2_g[l]),
                               sh2, sc2, router_pad, n_exp)
        res_x = (_moe(h2, aff, exp_w1[l], exp_w3[l], exp_w2[l]), g2)

    return _final(xs, res_x[0], res_x[1], row(final_norm_g))[None]
```

```python
import functools

import jax
import jax.numpy as jnp
from jax import lax
from jax.experimental import pallas as pl
from jax.experimental.pallas import tpu as pltpu

F32 = jnp.float32
BF16 = jnp.bfloat16
HIGHEST = lax.Precision.HIGHEST

EPS = 1e-6
LOG2E = 1.4426950408889634
GRID_W = 64
ROPE_THETA = 10000.0
CAPACITY_FACTOR = 2
GROUP_W = 256
HEAD_DIM = 64
CHUNK = 128
N_IN_BLOCKS = 23

LANES = 128
SUBLANES = 8
VMEM_LIMIT = 56 * 1024 * 1024

ROUTE_BLOCKS = 128
ROUTE_ITERS = 48
GATHER_WIN = 136

NT_DIMS = (((1,), (1,)), ((), ()))


def _cparams(sem, vmem=None):
    return pltpu.CompilerParams(dimension_semantics=sem, vmem_limit_bytes=vmem or VMEM_LIMIT)


def _sigmoid(x):
    return 1.0 / (1.0 + jnp.exp(-x))


def _iota(shape, dim):
    return lax.broadcasted_iota(jnp.int32, shape, dim)


def _mod_kernel(cc_ref, w_ref, b_ref, o_ref):
    cc = cc_ref[...]
    s = cc * _sigmoid(cc)
    w = w_ref[0]
    b = b_ref[0]
    o_ref[0, 0:1, :] = jnp.sum(w * s[:, 0:1], axis=0, keepdims=True) + b
    o_ref[0, 1:2, :] = jnp.sum(w * s[:, 1:2], axis=0, keepdims=True) + b


def _modulation(cc, ada_w, ada_b):
    depth, d, m = ada_w.shape
    tn = 1536
    return pl.pallas_call(
        _mod_kernel,
        grid=(depth, m // tn),
        in_specs=[pl.BlockSpec((d, 2), lambda l, j: (0, 0)),
                  pl.BlockSpec((1, d, tn), lambda l, j: (l, 0, j)),
                  pl.BlockSpec((1, 1, tn), lambda l, j: (l, 0, j))],
        out_specs=pl.BlockSpec((1, 2, tn), lambda l, j: (l, 0, j)),
        out_shape=jax.ShapeDtypeStruct((depth, 2, m), F32),
        compiler_params=_cparams(("arbitrary", "arbitrary")),
        name="modulation",
    )(cc, ada_w, ada_b.reshape(depth, 1, m))


def _inproj_kernel(*refs, has_res):
    if has_res:
        x_ref, y_ref, g2_ref, ng_ref, sh_ref, sc_ref, w_ref, z_ref, xr_ref = refs
        x = x_ref[...] + g2_ref[...] * y_ref[...]
        xr_ref[...] = x
    else:
        x_ref, ng_ref, sh_ref, sc_ref, w_ref, z_ref = refs
        x = x_ref[...]
    ms = jnp.mean(x * x, axis=-1, keepdims=True)
    h = x * lax.rsqrt(ms + EPS) * ng_ref[...]
    h = h * (1.0 + sc_ref[...]) + sh_ref[...]
    z_ref[...] = jnp.dot(h.astype(BF16), w_ref[...], preferred_element_type=F32)


def _inproj(x, res, norm_g, shift, scale, w_in):
    n, d = x.shape
    dz = w_in.shape[1]
    tn = min(n, 512)
    row = pl.BlockSpec((tn, d), lambda i: (i, 0))
    vec = pl.BlockSpec((1, d), lambda i: (0, 0))
    wsp = pl.BlockSpec((d, dz), lambda i: (0, 0))
    zsp = pl.BlockSpec((tn, dz), lambda i: (i, 0))
    if res is None:
        z = pl.pallas_call(
            functools.partial(_inproj_kernel, has_res=False),
            grid=(n // tn,),
            in_specs=[row, vec, vec, vec, wsp],
            out_specs=zsp,
            out_shape=jax.ShapeDtypeStruct((n, dz), F32),
            compiler_params=_cparams(("arbitrary",)),
            name="inproj",
        )(x, norm_g, shift, scale, w_in)
        return z, x
    y, g2 = res
    z, xr = pl.pallas_call(
        functools.partial(_inproj_kernel, has_res=True),
        grid=(n // tn,),
        in_specs=[row, row, vec, vec, vec, vec, wsp],
        out_specs=[zsp, row],
        out_shape=[jax.ShapeDtypeStruct((n, dz), F32), jax.ShapeDtypeStruct((n, d), F32)],
        compiler_params=_cparams(("arbitrary",)),
        name="inproj_res",
    )(x, y, g2, norm_g, shift, scale, w_in)
    return z, xr


def _local_kernel(cx_ref, cb_ref, cc_ref, u_ref, v_ref, px_ref, pc_ref, nx_ref, nc_ref,
                  cw_ref, g_ref, ws_ref, bs_ref, conv_ref, cmlp_ref, *, tn):
    i = pl.program_id(0)
    last = pl.num_programs(0) - 1
    cx = cc_ref[...] * cx_ref[...]
    prev = (pc_ref[...] * px_ref[...])[SUBLANES - 1:SUBLANES, :]
    nxt = (nc_ref[...] * nx_ref[...])[0:1, :]
    prev = jnp.where(i == 0, 0.0, prev)
    nxt = jnp.where(i == last, 0.0, nxt)
    row = _iota((tn, 1), 0)
    up = jnp.where(row == 0, prev, pltpu.roll(cx, 1, 0))
    dn = jnp.where(row == tn - 1, nxt, pltpu.roll(cx, tn - 1, 0))
    w = cw_ref[...]
    conv_ref[...] = cb_ref[...] * (w[0:1] * up + w[1:2] * cx + w[2:3] * dn)

    v = v_ref[...]
    vn = v * lax.rsqrt(jnp.mean(v * v, axis=-1, keepdims=True) + EPS) * g_ref[...]
    head = _iota((1, GROUP_W), 1) >> 6
    for c in range(tn // CHUNK):
        sl = slice(c * CHUNK, (c + 1) * CHUNK)
        vc = vn[sl]
        acc = bs_ref[...]
        for h in range(GROUP_W // HEAD_DIM):
            vm = jnp.where(head == h, vc, 0.0).astype(BF16)
            acc = acc + jnp.dot(ws_ref[h], vm, preferred_element_type=F32)
        cmlp_ref[sl, :] = u_ref[sl, :] * acc


def _local_mixers(z, conv_w, cmlp_g, cmlp_ws, cmlp_bias):
    n = z.shape[0]
    tn = min(n, 256)
    tb = tn // SUBLANES
    nb8 = n // SUBLANES

    def col(j):
        return pl.BlockSpec((tn, GROUP_W), lambda i: (i, j))

    def prev(j):
        return pl.BlockSpec((SUBLANES, GROUP_W), lambda i: (jnp.maximum(i * tb - 1, 0), j))

    def nxt(j):
        return pl.BlockSpec((SUBLANES, GROUP_W), lambda i: (jnp.minimum((i + 1) * tb, nb8 - 1), j))

    out = pl.BlockSpec((tn, GROUP_W), lambda i: (i, 0))
    return pl.pallas_call(
        functools.partial(_local_kernel, tn=tn),
        grid=(n // tn,),
        in_specs=[col(0), col(1), col(2), col(3), col(4), prev(0), prev(2), nxt(0), nxt(2),
                  pl.BlockSpec((3, GROUP_W), lambda i: (0, 0)),
                  pl.BlockSpec((1, GROUP_W), lambda i: (0, 0)),
                  pl.BlockSpec((4, CHUNK, CHUNK), lambda i: (0, 0, 0)),
                  pl.BlockSpec((CHUNK, GROUP_W), lambda i: (0, 0))],
        out_specs=[out, out],
        out_shape=[jax.ShapeDtypeStruct((n, GROUP_W), F32)] * 2,
        compiler_params=_cparams(("arbitrary",)),
        name="local_mixers",
    )(z, z, z, z, z, z, z, z, z, conv_w, cmlp_g, cmlp_ws, cmlp_bias)


def _attprep_kernel(*refs, rope):
    if rope:
        q_ref, kv_ref, qg_ref, kg_ref, cos_ref, sin_ref, qo_ref, kt_ref, vd_ref = refs
    else:
        q_ref, kv_ref, qg_ref, kg_ref, qo_ref, kt_ref, vd_ref = refs
    q = q_ref[...]
    kv = kv_ref[...]
    k = kv[:, :LANES]
    v = kv[:, LANES:]
    same_head = (_iota((LANES, LANES), 0) >> 6) == (_iota((LANES, LANES), 1) >> 6)
    bd = jnp.where(same_head, 1.0 / HEAD_DIM, 0.0)
    lane = _iota((1, LANES), 1)
    first_half = (lane & 31) < 16

    def head_norm(x, g):
        ms = jnp.dot(x * x, bd, precision=HIGHEST, preferred_element_type=F32)
        return x * lax.rsqrt(ms + EPS) * g

    def rotary(x):
        if not rope:
            return x
        rot = jnp.where(first_half, -pltpu.roll(x, LANES - 16, 1), pltpu.roll(x, 16, 1))
        return x * cos_ref[...] + rot * sin_ref[...]

    for half in range(2):
        sl = slice(half * LANES, (half + 1) * LANES)
        qh = rotary(head_norm(q[:, sl], qg_ref[...]))
        qo_ref[:, sl] = (qh * (HEAD_DIM ** -0.5 * LOG2E)).astype(BF16)
    kn = rotary(head_norm(k, kg_ref[...]))
    kt = kn.T.astype(BF16)
    kt_ref[0:64, :] = kt[0:64]
    kt_ref[64:128, :] = kt[0:64]
    kt_ref[128:192, :] = kt[64:128]
    kt_ref[192:256, :] = kt[64:128]
    vr = pltpu.roll(v, HEAD_DIM, 1)
    low = lane < HEAD_DIM
    vd_ref[:, 0:LANES] = jnp.where(low, v, vr).astype(BF16)
    vd_ref[:, LANES:] = jnp.where(low, vr, v).astype(BF16)


def _attprep(z, q_g, k_g, tables):
    n = z.shape[0]
    tn = min(n, 256)
    rope = tables is not None
    in_specs = [pl.BlockSpec((tn, GROUP_W), lambda i: (i, 5)),
                pl.BlockSpec((tn, GROUP_W), lambda i: (i, 6)),
                pl.BlockSpec((1, LANES), lambda i: (0, 0)),
                pl.BlockSpec((1, LANES), lambda i: (0, 0))]
    args = [z, z, q_g, k_g]
    if rope:
        in_specs += [pl.BlockSpec((tn, LANES), lambda i: (i, 0))] * 2
        args += list(tables)
    return pl.pallas_call(
        functools.partial(_attprep_kernel, rope=rope),
        grid=(n // tn,),
        in_specs=in_specs,
        out_specs=[pl.BlockSpec((tn, GROUP_W), lambda i: (i, 0)),
                   pl.BlockSpec((GROUP_W, tn), lambda i: (0, i)),
                   pl.BlockSpec((tn, GROUP_W), lambda i: (i, 0))],
        out_shape=[jax.ShapeDtypeStruct((n, GROUP_W), BF16),
                   jax.ShapeDtypeStruct((GROUP_W, n), BF16),
                   jax.ShapeDtypeStruct((n, GROUP_W), BF16)],
        compiler_params=_cparams(("arbitrary",)),
        name="attprep_rope" if rope else "attprep",
    )(*args)


def _flash_kernel(q_ref, k_ref, v_ref, o_ref, s_ref, mx_ref, *, tk, nk):
    q = q_ref[...]
    tq = q.shape[0]
    low = _iota((1, LANES), 1) < HEAD_DIM
    zero = jnp.zeros_like(q)
    qa = jnp.where(low, q, zero)
    qb = jnp.where(low, zero, q)

    def scores(j, slot):
        k = k_ref[:, pl.ds(pl.multiple_of(j * tk, LANES), tk)]
        for h, qh in enumerate((qa, qb)):
            s = jnp.dot(qh, k, preferred_element_type=F32)
            s_ref[slot, h] = s
            mx_ref[slot, h] = jnp.max(s, axis=1, keepdims=True)

    def consume(j, slot, carry):
        m_a, l_a, m_b, l_b, acc = carry
        v = v_ref[pl.ds(pl.multiple_of(j * tk, LANES), tk), :]
        zv = jnp.zeros_like(v)
        s_a = s_ref[slot, 0]
        s_b = s_ref[slot, 1]
        n_a = jnp.maximum(m_a, mx_ref[slot, 0])
        n_b = jnp.maximum(m_b, mx_ref[slot, 1])
        al_a = jnp.exp2(m_a - n_a)
        al_b = jnp.exp2(m_b - n_b)
        p_a = jnp.exp2(s_a - n_a)
        p_b = jnp.exp2(s_b - n_b)
        l_a = al_a * l_a + jnp.sum(p_a, axis=1, keepdims=True)
        l_b = al_b * l_b + jnp.sum(p_b, axis=1, keepdims=True)
        pv = (jnp.dot(p_a.astype(BF16), jnp.where(low, v, zv), preferred_element_type=F32)
              + jnp.dot(p_b.astype(BF16), jnp.where(low, zv, v), preferred_element_type=F32))
        acc = acc * jnp.where(low, al_a, al_b) + pv
        return n_a, l_a, n_b, l_b, acc

    neg = jnp.full((tq, 1), -jnp.inf, F32)
    zl = jnp.zeros((tq, 1), F32)
    carry = (neg, zl, neg, zl, jnp.zeros((tq, LANES), F32))
    scores(0, 0)
    if nk % 2 == 0 and nk >= 4:
        def pair(i, carry):
            j = 2 * i
            scores(j + 1, 1)
            carry = consume(j, 0, carry)
            scores(j + 2, 0)
            return consume(j + 1, 1, carry)

        carry = lax.fori_loop(0, nk // 2 - 1, pair, carry)
        scores(nk - 1, 1)
        carry = consume(nk - 2, 0, carry)
        carry = consume(nk - 1, 1, carry)
    else:
        for j in range(nk):
            if j + 1 < nk:
                scores(j + 1, (j + 1) % 2)
            carry = consume(j, j % 2, carry)
    _, l_a, _, l_b, acc = carry
    o_ref[...] = acc / jnp.where(low, l_a, l_b)


def _kv_tile(nk):
    best = LANES
    for t in range(LANES, 2048 + 1, LANES):
        if nk % t == 0:
            best = t
    return best


def _flash(q, kt, vd):
    n = q.shape[0]
    nk = kt.shape[1]
    tq = min(n, 512)
    tk = _kv_tile(nk)
    return pl.pallas_call(
        functools.partial(_flash_kernel, tk=tk, nk=nk // tk),
        grid=(2, n // tq),
        in_specs=[pl.BlockSpec((tq, LANES), lambda g, i: (i, g)),
                  pl.BlockSpec((LANES, nk), lambda g, i: (g, 0)),
                  pl.BlockSpec((nk, LANES), lambda g, i: (0, g))],
        out_specs=pl.BlockSpec((tq, LANES), lambda g, i: (i, g)),
        out_shape=jax.ShapeDtypeStruct((n, GROUP_W), F32),
        scratch_shapes=[pltpu.VMEM((2, 2, tq, tk), F32), pltpu.VMEM((2, 2, tq, 1), F32)],
        compiler_params=_cparams(("arbitrary", "arbitrary")),
        name="flash_attention",
    )(q, kt, vd)


def _mlstm_kernel(*refs, fwd, merge):
    if merge:
        (q_ref, k_ref, v_ref, zg_ref, gb_ref, c0_ref, n0_ref, m0_ref, hf_ref, po_ref, ng_ref,
         h_ref, c_ref, n_ref, m_ref) = refs
    else:
        (q_ref, k_ref, v_ref, zg_ref, gb_ref, c0_ref, n0_ref, m0_ref,
         h_ref, c_ref, n_ref, m_ref) = refs

    @pl.when(pl.program_id(0) == 0)
    def _():
        c_ref[...] = c0_ref[...]
        n_ref[...] = n0_ref[...]
        m_ref[...] = m0_ref[...]

    off = 0 if fwd else 4
    nh = GROUP_W // HEAD_DIM
    q = q_ref[...]
    ks = k_ref[...] * (HEAD_DIM ** -0.5)
    v = v_ref[...]
    g = zg_ref[...] + gb_ref[...]
    lane = _iota((1, LANES), 1)
    lsig = jnp.minimum(g, 0.0) - jnp.log(1.0 + jnp.exp(-jnp.abs(g)))
    gcol = jnp.where((lane >= 8) & (lane < 16), lsig, g)
    lfcol = pltpu.roll(gcol, LANES - 8, 1)
    rr = _iota((CHUNK, CHUNK), 0)
    cc = _iota((CHUNK, CHUNK), 1)
    sees = (cc <= rr) if fwd else (cc >= rr)
    tri = jnp.where(sees, 1.0, 0.0)
    bc = jnp.dot(tri, lfcol, precision=HIGHEST, preferred_element_type=F32)
    bct = bc.T
    gt = gcol.T
    total = bc[CHUNK - 1:CHUNK, :] if fwd else bc[0:1, :]
    cp = c_ref[...]
    npv = n_ref[0:1, :]
    mp = m_ref[0:1, :]
    head = _iota((1, GROUP_W), 1) >> 6
    qb = q.astype(BF16)
    kb = ks.astype(BF16)
    qc = lax.dot_general(qb, cp.astype(BF16), NT_DIMS, preferred_element_type=F32)
    eh = jnp.where((_iota((GROUP_W, LANES), 0) >> 6) == _iota((GROUP_W, LANES), 1), 1.0, 0.0)
    nq = jnp.dot((q * npv).astype(BF16), eh.astype(BF16), preferred_element_type=F32)

    num = jnp.zeros((CHUNK, GROUP_W), F32)
    wi_x = jnp.zeros((CHUNK, GROUP_W), F32)
    den_x = jnp.ones((CHUNK, GROUP_W), F32)
    for h in range(nh):
        ch = off + h
        bc_t = bc[:, ch:ch + 1]
        dm = jnp.where(sees, bc_t - bct[ch:ch + 1, :] + gt[ch:ch + 1, :], -jnp.inf)
        inter = bc_t + mp[:, ch:ch + 1]
        m_t = jnp.maximum(inter, jnp.max(dm, axis=1, keepdims=True))
        hm = head == h
        qh = jnp.where(hm, q, 0.0).astype(BF16)
        s = lax.dot_general(qh, kb, NT_DIMS, preferred_element_type=F32) * jnp.exp(dm - m_t)
        wi = jnp.exp(inter - m_t)
        vh = jnp.where(hm, v, 0.0).astype(BF16)
        num = num + jnp.dot(s.astype(BF16), vh, preferred_element_type=F32)
        den = jnp.sum(s, axis=1, keepdims=True) + wi * nq[:, h:h + 1]
        dd = jnp.maximum(jnp.abs(den), jnp.exp(-m_t))
        wi_x = jnp.where(hm, wi, wi_x)
        den_x = jnp.where(hm, dd, den_x)
    hout = (num + wi_x * qc) / den_x

    a = total - bc + gcol
    m_loc = jnp.max(a, axis=0, keepdims=True)
    w = jnp.exp(a - m_loc)
    m_new = jnp.maximum(total + mp, m_loc)
    d_old = jnp.exp(total + mp - m_new)
    d_loc = jnp.exp(m_loc - m_new)
    w_x = jnp.zeros((CHUNK, GROUP_W), F32)
    dold_x = jnp.zeros((1, GROUP_W), F32)
    dloc_x = jnp.zeros((1, GROUP_W), F32)
    for h in range(nh):
        ch = off + h
        hm = head == h
        w_x = jnp.where(hm, w[:, ch:ch + 1], w_x)
        dold_x = jnp.where(hm, d_old[:, ch:ch + 1], dold_x)
        dloc_x = jnp.where(hm, d_loc[:, ch:ch + 1], dloc_x)
    c_loc = jnp.dot((w_x * v).T.astype(BF16), kb, preferred_element_type=F32)
    same_head = (_iota((GROUP_W, GROUP_W), 0) >> 6) == (_iota((GROUP_W, GROUP_W), 1) >> 6)
    c_ref[...] = cp * dold_x + jnp.where(same_head, c_loc, 0.0) * dloc_x
    n_new = npv * dold_x + jnp.sum(w_x * ks, axis=0, keepdims=True) * dloc_x
    n_ref[...] = jnp.broadcast_to(n_new, n_ref.shape)
    m_ref[...] = jnp.broadcast_to(m_new, m_ref.shape)

    if merge:
        hs = hf_ref[...] + hout
        bd = jnp.where(same_head, 1.0 / HEAD_DIM, 0.0)
        ms = jnp.dot(hs * hs, bd, precision=HIGHEST, preferred_element_type=F32)
        h_ref[...] = _sigmoid(po_ref[...]) * (hs * lax.rsqrt(ms + EPS) * ng_ref[...])
    else:
        h_ref[...] = hout


def _mlstm(z, gate_b, state, *, fwd, merge=None):
    n = z.shape[0]
    nc = n // CHUNK
    order = (lambda c: c) if fwd else (lambda c: nc - 1 - c)

    def col(j):
        return pl.BlockSpec((CHUNK, GROUP_W), lambda c: (order(c), j))

    def full(shape):
        return pl.BlockSpec(shape, lambda c: (0,) * len(shape))

    st_shapes = [(GROUP_W, GROUP_W), (SUBLANES, GROUP_W), (SUBLANES, LANES)]
    in_specs = [col(7), col(8), col(9),
                pl.BlockSpec((CHUNK, LANES), lambda c: (order(c), N_IN_BLOCKS - 1)),
                full((1, LANES))] + [full(s) for s in st_shapes]
    args = [z, z, z, z, gate_b, *state]
    if merge is not None:
        hf, norm_g = merge
        in_specs += [pl.BlockSpec((CHUNK, GROUP_W), lambda c: (order(c), 0)), col(10), full((1, GROUP_W))]
        args += [hf, z, norm_g]
    outs = pl.pallas_call(
        functools.partial(_mlstm_kernel, fwd=fwd, merge=merge is not None),
        grid=(nc,),
        in_specs=in_specs,
        out_specs=[pl.BlockSpec((CHUNK, GROUP_W), lambda c: (order(c), 0))] + [full(s) for s in st_shapes],
        out_shape=[jax.ShapeDtypeStruct((n, GROUP_W), F32)] + [jax.ShapeDtypeStruct(s, F32) for s in st_shapes],
        compiler_params=_cparams(("arbitrary",)),
        name="mlstm_fwd" if fwd else "mlstm_bwd",
    )(*args)
    return outs[0], tuple(outs[1:])


def _outproj_kernel(cv_ref, cm_ref, at_ref, ml_ref, w_ref, x_ref, g1_ref, ng_ref, sh_ref, sc_ref, rw_ref,
                    xo_ref, h_ref, aff_ref, *, n_exp):
    mix = jnp.zeros(x_ref.shape, F32)
    for j, ref in enumerate((cv_ref, cm_ref, at_ref, ml_ref)):
        mix = mix + jnp.dot(ref[...].astype(BF16), w_ref[j * GROUP_W:(j + 1) * GROUP_W, :],
                            preferred_element_type=F32)
    x = x_ref[...] + g1_ref[...] * mix
    xo_ref[...] = x
    ms = jnp.mean(x * x, axis=-1, keepdims=True)
    h = x * lax.rsqrt(ms + EPS) * ng_ref[...]
    h = h * (1.0 + sc_ref[...]) + sh_ref[...]
    h_ref[...] = h.astype(BF16)
    logits = jnp.dot(h, rw_ref[...], precision=HIGHEST, preferred_element_type=F32)
    lt = logits.T[0:n_exp, :]
    e = jnp.exp(lt - jnp.max(lt, axis=0, keepdims=True))
    aff_ref[...] = e / jnp.sum(e, axis=0, keepdims=True)


def _outproj(mixers, w_out, x, gate1, norm_g, shift, scale, router_pad, n_exp):
    n, d = x.shape
    tn = min(n, 512)
    mix_spec = pl.BlockSpec((tn, GROUP_W), lambda i: (i, 0))
    row = pl.BlockSpec((tn, d), lambda i: (i, 0))
    vec = pl.BlockSpec((1, d), lambda i: (0, 0))
    return pl.pallas_call(
        functools.partial(_outproj_kernel, n_exp=n_exp),
        grid=(n // tn,),
        in_specs=[mix_spec] * 4 + [pl.BlockSpec((d, d), lambda i: (0, 0)), row, vec, vec, vec, vec,
                                   pl.BlockSpec((d, LANES), lambda i: (0, 0))],
        out_specs=[row, row, pl.BlockSpec((n_exp, tn), lambda i: (0, i))],
        out_shape=[jax.ShapeDtypeStruct((n, d), F32), jax.ShapeDtypeStruct((n, d), BF16),
                   jax.ShapeDtypeStruct((n_exp, n), F32)],
        compiler_params=_cparams(("arbitrary",)),
        name="outproj_router",
    )(*mixers, w_out, x, gate1, norm_g, shift, scale, router_pad)


def _route_kernel(aff_ref, rank_ref, offs_ref, *, cap, iters):
    aff = aff_ref[...]
    n_exp = aff.shape[0]

    def count(mask):
        c = jnp.where(mask, 1.0, 0.0)
        return jnp.sum(jnp.sum(c, axis=1, keepdims=True), axis=2, keepdims=True)

    def bisect(_, carry):
        lo, hi = carry
        mid = 0.5 * (lo + hi)
        ok = count(aff >= mid) >= cap
        return jnp.where(ok, mid, lo), jnp.where(ok, hi, mid)

    _, hi = lax.fori_loop(0, iters, bisect,
                          (jnp.zeros((n_exp, 1, 1), F32), jnp.full((n_exp, 1, 1), 2.0, F32)))
    below = jnp.where(aff < hi, aff, -1.0)
    thr = jnp.max(jnp.max(below, axis=1, keepdims=True), axis=2, keepdims=True)
    gt = aff > thr
    eq = aff == thr
    need = cap - count(gt)

    rr = _iota((LANES, LANES), 0)
    cc = _iota((LANES, LANES), 1)
    upper = jnp.where(rr <= cc, 1.0, 0.0).astype(BF16)
    lower = jnp.where(cc < rr, 1.0, 0.0).astype(BF16)

    def prefix(x01):
        within = jnp.dot(x01.astype(BF16), upper, preferred_element_type=F32)
        offs = jnp.dot(lower, within.astype(BF16), preferred_element_type=F32)[:, LANES - 1:LANES]
        return within - x01 + offs, offs

    offs_all = jnp.zeros((ROUTE_BLOCKS, LANES), F32)
    for e in range(n_exp):
        gt_e = jnp.where(gt[e], 1.0, 0.0)
        eq_e = jnp.where(eq[e], 1.0, 0.0)
        eq_rank, _ = prefix(eq_e)
        sel = jnp.maximum(gt_e, jnp.where(eq_rank < need[e], eq_e, 0.0))
        rk, offs = prefix(sel)
        rk = jnp.where(sel > 0.0, rk, -1.0)
        rank_ref[e] = rk.astype(jnp.int32)
        offs_all = jnp.where(cc == e, offs, offs_all)
    offs_ref[...] = offs_all.astype(jnp.int32)


def _route(aff_t, cap):
    n_exp, n = aff_t.shape
    assert n % LANES == 0 and n <= ROUTE_BLOCKS * LANES and cap <= n
    a = aff_t.reshape(n_exp, n // LANES, LANES)
    if n < ROUTE_BLOCKS * LANES:
        a = jnp.pad(a, ((0, 0), (0, ROUTE_BLOCKS - n // LANES), (0, 0)), constant_values=-1.0)
    blk = pl.BlockSpec((n_exp, ROUTE_BLOCKS, LANES), lambda i: (0, 0, 0))
    rank, offs = pl.pallas_call(
        functools.partial(_route_kernel, cap=cap, iters=ROUTE_ITERS),
        grid=(1,),
        in_specs=[blk],
        out_specs=[blk, pl.BlockSpec((ROUTE_BLOCKS, LANES), lambda i: (0, 0))],
        out_shape=[jax.ShapeDtypeStruct((n_exp, ROUTE_BLOCKS, LANES), jnp.int32),
                   jax.ShapeDtypeStruct((ROUTE_BLOCKS, LANES), jnp.int32)],
        compiler_params=_cparams(("arbitrary",)),
        name="route",
    )(a)
    offs_flat = offs[:, :n_exp].T.reshape(-1)
    return a, rank, offs_flat


def _gather_kernel(offs_ref, h_ref, rank_ref, gate_ref, xs_ref, gs_ref, idx_ref, acc_ref, gacc_ref, iacc_ref,
                   *, cap, spb):
    e = pl.program_id(0)
    sb = pl.program_id(1)

    @pl.when(sb == 0)
    def _():
        acc_ref[...] = jnp.zeros(acc_ref.shape, F32)
        gacc_ref[...] = jnp.zeros(gacc_ref.shape, F32)
        iacc_ref[...] = jnp.zeros(iacc_ref.shape, F32)

    rows = _iota((GATHER_WIN, LANES), 0)
    lane = _iota((1, LANES), 1)

    def body(k, carry):
        b = sb * spb + k
        r0 = offs_ref[e * ROUTE_BLOCKS + b]
        r0a = pl.multiple_of((r0 >> 3) << 3, SUBLANES)
        rrow = rank_ref[0, pl.ds(b, 1), :]
        grow = gate_ref[0, pl.ds(b, 1), :]
        trow = (b * LANES + lane).astype(F32)
        hit = (rrow - r0a) == rows
        hblk = h_ref[pl.ds(pl.multiple_of(k * LANES, LANES), LANES), :]
        onehot = jnp.where(hit, 1.0, 0.0).astype(BF16)
        win = pl.ds(r0a, GATHER_WIN)
        acc_ref[win, :] += jnp.dot(onehot, hblk, preferred_element_type=F32)
        gacc_ref[win, :] += jnp.sum(jnp.where(hit, grow, 0.0), axis=1, keepdims=True)
        iacc_ref[win, :] += jnp.sum(jnp.where(hit, trow, 0.0), axis=1, keepdims=True)
        return carry

    lax.fori_loop(0, spb, body, 0)

    @pl.when(sb == pl.num_programs(1) - 1)
    def _():
        xs_ref[0] = acc_ref[0:cap, :].astype(BF16)
        gs_ref[0] = gacc_ref[0:cap, :]
        idx_ref[0] = iacc_ref[0:cap, :].astype(jnp.int32)


def _gather(offs_flat, h2, rank, gate, cap):
    n, d = h2.shape
    n_exp = rank.shape[0]
    sbt = min(n, 2048)
    spb = sbt // LANES
    blk = pl.BlockSpec((1, ROUTE_BLOCKS, LANES), lambda e, s, offs: (e, 0, 0))
    return pl.pallas_call(
        functools.partial(_gather_kernel, cap=cap, spb=spb),
        grid_spec=pltpu.PrefetchScalarGridSpec(
            num_scalar_prefetch=1,
            grid=(n_exp, n // sbt),
            in_specs=[pl.BlockSpec((sbt, d), lambda e, s, offs: (s, 0)), blk, blk],
            out_specs=[pl.BlockSpec((1, cap, d), lambda e, s, offs: (e, 0, 0)),
                       pl.BlockSpec((1, cap, 1), lambda e, s, offs: (e, 0, 0)),
                       pl.BlockSpec((1, cap, 1), lambda e, s, offs: (e, 0, 0))],
            scratch_shapes=[pltpu.VMEM((cap + GATHER_WIN, d), F32),
                            pltpu.VMEM((cap + GATHER_WIN, 1), F32),
                            pltpu.VMEM((cap + GATHER_WIN, 1), F32)]),
        out_shape=[jax.ShapeDtypeStruct((n_exp, cap, d), BF16),
                   jax.ShapeDtypeStruct((n_exp, cap, 1), F32),
                   jax.ShapeDtypeStruct((n_exp, cap, 1), jnp.int32)],
        compiler_params=_cparams(("arbitrary", "arbitrary")),
        name="moe_gather",
    )(offs_flat, h2, rank, gate)


def _ffn_kernel(xs_ref, w1_ref, w3_ref, w2_ref, gs_ref, o_ref, *, cap, tm):
    f = pl.program_id(1)

    @pl.when(f == 0)
    def _():
        o_ref[...] = jnp.zeros(o_ref.shape, F32)

    w1 = w1_ref[0, 0].astype(BF16)
    w3 = w3_ref[0, 0].astype(BF16)
    w2 = w2_ref[0, 0].astype(BF16)
    for r in range(cap // tm):
        sl = slice(r * tm, (r + 1) * tm)
        x = xs_ref[0, sl, :]
        a = jnp.dot(x, w1, preferred_element_type=F32)
        b = jnp.dot(x, w3, preferred_element_type=F32)
        hid = (a * _sigmoid(a) * b).astype(BF16)
        o_ref[0, sl, :] += jnp.dot(hid, w2, preferred_element_type=F32)

    @pl.when(f == pl.num_programs(1) - 1)
    def _():
        o_ref[0] = o_ref[0] * gs_ref[0]


def _ffn(xs, gs, w1, w3, w2, layer):
    n_exp, cap, d = xs.shape
    ff = w1.shape[3]
    tf = 512
    tm = min(cap, 512)
    return pl.pallas_call(
        functools.partial(_ffn_kernel, cap=cap, tm=tm),
        grid=(n_exp, ff // tf),
        in_specs=[pl.BlockSpec((1, cap, d), lambda e, f: (e, 0, 0)),
                  pl.BlockSpec((1, 1, d, tf), lambda e, f: (layer, e, 0, f)),
                  pl.BlockSpec((1, 1, d, tf), lambda e, f: (layer, e, 0, f)),
                  pl.BlockSpec((1, 1, tf, d), lambda e, f: (layer, e, f, 0)),
                  pl.BlockSpec((1, cap, 1), lambda e, f: (e, 0, 0))],
        out_specs=pl.BlockSpec((1, cap, d), lambda e, f: (e, 0, 0)),
        out_shape=jax.ShapeDtypeStruct((n_exp, cap, d), F32),
        compiler_params=_cparams(("arbitrary", "arbitrary")),
        name="moe_ffn",
    )(xs, w1, w3, w2, gs)


def _combine_kernel(offs_ref, idx_ref, out_ref, y_ref, *, cap, tht):
    th = pl.program_id(0)
    e = pl.program_id(1)

    @pl.when(e == 0)
    def _():
        y_ref[...] = jnp.zeros(y_ref.shape, F32)

    bpt = tht // LANES
    first = e * ROUTE_BLOCKS + th * bpt
    start = offs_ref[first]
    is_last = th == pl.num_programs(0) - 1
    end = jnp.where(is_last, cap, offs_ref[jnp.minimum(first + bpt, (e + 1) * ROUTE_BLOCKS - 1)])
    base = th * tht

    def body(r, carry):
        t = jnp.clip(idx_ref[e * cap + r] - base, 0, tht - 1)
        y_ref[pl.ds(t, 1), :] += out_ref[0, pl.ds(r, 1), :]
        return carry

    lax.fori_loop(start, end, body, 0)


def _combine(offs_flat, idx_flat, out, n):
    n_exp, cap, d = out.shape
    tht = min(n, 4096)
    return pl.pallas_call(
        functools.partial(_combine_kernel, cap=cap, tht=tht),
        grid_spec=pltpu.PrefetchScalarGridSpec(
            num_scalar_prefetch=2,
            grid=(n // tht, n_exp),
            in_specs=[pl.BlockSpec((1, cap, d), lambda t, e, offs, idx: (e, 0, 0))],
            out_specs=pl.BlockSpec((tht, d), lambda t, e, offs, idx: (t, 0))),
        out_shape=jax.ShapeDtypeStruct((n, d), F32),
        compiler_params=_cparams(("arbitrary", "arbitrary")),
        name="moe_combine",
    )(offs_flat, idx_flat, out)


def _moe(h2, aff_t, w1, w3, w2, layer):
    n = h2.shape[0]
    n_exp = aff_t.shape[0]
    cap = CAPACITY_FACTOR * n // n_exp
    gate, rank, offs_flat = _route(aff_t, cap)
    xs, gs, idx = _gather(offs_flat, h2, rank, gate, cap)
    out = _ffn(xs, gs, w1, w3, w2, layer)
    return _combine(offs_flat, idx.reshape(-1), out, n)


def _final_kernel(x_ref, y_ref, g2_ref, ng_ref, o_ref):
    x = x_ref[...] + g2_ref[...] * y_ref[...]
    ms = jnp.mean(x * x, axis=-1, keepdims=True)
    o_ref[...] = x * lax.rsqrt(ms + EPS) * ng_ref[...]


def _final(x, y, g2, norm_g):
    n, d = x.shape
    tn = min(n, 512)
    row = pl.BlockSpec((tn, d), lambda i: (i, 0))
    vec = pl.BlockSpec((1, d), lambda i: (0, 0))
    return pl.pallas_call(
        _final_kernel,
        grid=(n // tn,),
        in_specs=[row, row, vec, vec],
        out_specs=row,
        out_shape=jax.ShapeDtypeStruct((n, d), F32),
        compiler_params=_cparams(("arbitrary",)),
        name="final_norm",
    )(x, y, g2, norm_g)


def _rope_tables(n):
    pos = jnp.arange(n)
    row, colp = pos // GRID_W, pos % GRID_W
    axis_dim = HEAD_DIM // 2
    inv_freq = ROPE_THETA ** (-jnp.arange(0, axis_dim, 2, dtype=F32) / axis_dim)

    def axis_angles(p):
        a = p.astype(F32)[:, None] * inv_freq[None, :]
        return jnp.concatenate([a, a], axis=-1)

    ang = jnp.concatenate([axis_angles(row), axis_angles(colp)], axis=-1)
    ang = jnp.concatenate([ang, ang], axis=-1)
    return jnp.cos(ang), jnp.sin(ang)


def _zero_state():
    return (jnp.zeros((GROUP_W, GROUP_W), F32), jnp.zeros((SUBLANES, GROUP_W), F32),
            jnp.zeros((SUBLANES, LANES), F32))


def kernel(x, c, ctx, c_ctx, ada_w, ada_b, norm1_g, w_in, conv_w, cmlp_norm_g, cmlp_ws, cmlp_bs, q_norm_g, k_norm_g, ml_igate_b, ml_fgate_b, ml_norm_g, w_out, norm2_g, router_w, exp_w1, exp_w3, exp_w2, final_norm_g):
    assert x.shape[0] == 1 and ctx.shape[0] == 1
    depth, d, d_in = w_in.shape
    n_exp = router_w.shape[-1]
    assert d_in <= N_IN_BLOCKS * LANES and d == 4 * GROUP_W
    xs = x[0]
    xc = ctx[0]
    n = xs.shape[0]

    mod = _modulation(jnp.stack([c[0], c_ctx], axis=1), ada_w, ada_b)
    tables = _rope_tables(n)
    row = lambda v: v.reshape(1, -1)

    res_x = None
    res_c = None
    for l in range(depth):
        parts = [[mod[l, s:s + 1, j * d:(j + 1) * d] for j in range(6)] for s in range(2)]
        sh1, sc1, g1, sh2, sc2, g2 = parts[0]
        csh1, csc1, cg1, csh2, csc2, cg2 = parts[1]
        w_in_l = jnp.pad(w_in[l], ((0, 0), (0, N_IN_BLOCKS * LANES - d_in))).astype(BF16)
        w_out_l = w_out[l].astype(BF16)
        cmlp_ws_l = cmlp_ws[l].astype(BF16)
        cmlp_bias = jnp.repeat(cmlp_bs[l].T, HEAD_DIM, axis=1)
        qg = jnp.tile(q_norm_g[l], 2).reshape(1, LANES)
        kg = jnp.tile(k_norm_g[l], 2).reshape(1, LANES)
        gate_b = jnp.pad(jnp.concatenate([ml_igate_b[l], ml_fgate_b[l]]), (0, LANES - 16)).reshape(1, LANES)
        router_pad = jnp.pad(router_w[l], ((0, 0), (0, LANES - n_exp)))
        update_ctx = l < depth - 1

        zc, xc = _inproj(xc, res_c, row(norm1_g[l]), csh1, csc1, w_in_l)
        qc_, ktc, vdc = _attprep(zc, qg, kg, None)
        hcf, st_f = _mlstm(zc, gate_b, _zero_state(), fwd=True)
        ml_c, st_b = _mlstm(zc, gate_b, _zero_state(), fwd=False, merge=(hcf, row(ml_norm_g[l])))
        if update_ctx:
            conv_c, cmlp_c = _local_mixers(zc, conv_w[l], row(cmlp_norm_g[l]), cmlp_ws_l, cmlp_bias)
            att_c = _flash(qc_, ktc, vdc)
            xc, hc2, affc = _outproj((conv_c, cmlp_c, att_c, ml_c), w_out_l, xc, cg1, row(norm2_g[l]),
                                     csh2, csc2, router_pad, n_exp)
            res_c = (_moe(hc2, affc, exp_w1, exp_w3, exp_w2, l), cg2)

        z, xs = _inproj(xs, res_x, row(norm1_g[l]), sh1, sc1, w_in_l)
        conv_x, cmlp_x = _local_mixers(z, conv_w[l], row(cmlp_norm_g[l]), cmlp_ws_l, cmlp_bias)
        q_, kt, vd = _attprep(z, qg, kg, tables)
        att_x = _flash(q_, jnp.concatenate([ktc, kt], axis=1), jnp.concatenate([vdc, vd], axis=0))
        hf, _ = _mlstm(z, gate_b, st_f, fwd=True)
        ml_x, _ = _mlstm(z, gate_b, st_b, fwd=False, merge=(hf, row(ml_norm_g[l])))
        xs, h2, aff = _outproj((conv_x, cmlp_x, att_x, ml_x), w_out_l, xs, g1, row(norm2_g[l]),
                               sh2, sc2, router_pad, n_exp)
        res_x = (_moe(h2, aff, exp_w1, exp_w3, exp_w2, l), g2)

    return _final(xs, res_x[0], res_x[1], row(final_norm_g))[None]
```

```python
import functools

import jax
import jax.numpy as jnp
from jax import lax
from jax.experimental import pallas as pl
from jax.experimental.pallas import tpu as pltpu

F32 = jnp.float32
BF16 = jnp.bfloat16
HIGHEST = lax.Precision.HIGHEST

EPS = 1e-6
LOG2E = 1.4426950408889634
GRID_W = 64
ROPE_THETA = 10000.0
CAPACITY_FACTOR = 2
GROUP_W = 256
HEAD_DIM = 64
CHUNK = 128
N_IN_BLOCKS = 23

LANES = 128
SUBLANES = 8
VMEM_LIMIT = 56 * 1024 * 1024

ROUTE_BLOCKS = 128
ROUTE_ITERS = 48
GATHER_WIN = 136
COMBINE_GROUP = 4
FLASH_TQ = 256
FLASH_TK_MAX = 3328

NT_DIMS = (((1,), (1,)), ((), ()))


def _cparams(sem, vmem=None):
    return pltpu.CompilerParams(dimension_semantics=sem, vmem_limit_bytes=vmem or VMEM_LIMIT)


def _sigmoid(x):
    return 1.0 / (1.0 + jnp.exp(-x))


def _iota(shape, dim):
    return lax.broadcasted_iota(jnp.int32, shape, dim)


def _mod_kernel(cc_ref, w_ref, b_ref, o_ref):
    cc = cc_ref[...]
    s = cc * _sigmoid(cc)
    w = w_ref[0]
    b = b_ref[0]
    o_ref[0, 0:1, :] = jnp.sum(w * s[:, 0:1], axis=0, keepdims=True) + b
    o_ref[0, 1:2, :] = jnp.sum(w * s[:, 1:2], axis=0, keepdims=True) + b


def _modulation(cc, ada_w, ada_b):
    depth, d, m = ada_w.shape
    tn = 1536
    return pl.pallas_call(
        _mod_kernel,
        grid=(depth, m // tn),
        in_specs=[pl.BlockSpec((d, 2), lambda l, j: (0, 0)),
                  pl.BlockSpec((1, d, tn), lambda l, j: (l, 0, j)),
                  pl.BlockSpec((1, 1, tn), lambda l, j: (l, 0, j))],
        out_specs=pl.BlockSpec((1, 2, tn), lambda l, j: (l, 0, j)),
        out_shape=jax.ShapeDtypeStruct((depth, 2, m), F32),
        compiler_params=_cparams(("arbitrary", "arbitrary")),
        name="modulation",
    )(cc, ada_w, ada_b.reshape(depth, 1, m))


def _inproj_kernel(*refs, has_res):
    if has_res:
        x_ref, y_ref, g2_ref, ng_ref, sh_ref, sc_ref, w_ref, z_ref, xr_ref = refs
        x = x_ref[...] + g2_ref[...] * y_ref[...]
        xr_ref[...] = x
    else:
        x_ref, ng_ref, sh_ref, sc_ref, w_ref, z_ref = refs
        x = x_ref[...]
    ms = jnp.mean(x * x, axis=-1, keepdims=True)
    h = x * lax.rsqrt(ms + EPS) * ng_ref[...]
    h = h * (1.0 + sc_ref[...]) + sh_ref[...]
    z_ref[...] = jnp.dot(h.astype(BF16), w_ref[...], preferred_element_type=F32)


def _inproj(x, res, norm_g, shift, scale, w_in):
    n, d = x.shape
    dz = w_in.shape[1]
    tn = min(n, 512)
    row = pl.BlockSpec((tn, d), lambda i: (i, 0))
    vec = pl.BlockSpec((1, d), lambda i: (0, 0))
    wsp = pl.BlockSpec((d, dz), lambda i: (0, 0))
    zsp = pl.BlockSpec((tn, dz), lambda i: (i, 0))
    if res is None:
        z = pl.pallas_call(
            functools.partial(_inproj_kernel, has_res=False),
            grid=(n // tn,),
            in_specs=[row, vec, vec, vec, wsp],
            out_specs=zsp,
            out_shape=jax.ShapeDtypeStruct((n, dz), F32),
            compiler_params=_cparams(("arbitrary",)),
            name="inproj",
        )(x, norm_g, shift, scale, w_in)
        return z, x
    y, g2 = res
    z, xr = pl.pallas_call(
        functools.partial(_inproj_kernel, has_res=True),
        grid=(n // tn,),
        in_specs=[row, row, vec, vec, vec, vec, wsp],
        out_specs=[zsp, row],
        out_shape=[jax.ShapeDtypeStruct((n, dz), F32), jax.ShapeDtypeStruct((n, d), F32)],
        compiler_params=_cparams(("arbitrary",)),
        name="inproj_res",
    )(x, y, g2, norm_g, shift, scale, w_in)
    return z, xr


def _local_kernel(cx_ref, cb_ref, cc_ref, u_ref, v_ref, px_ref, pc_ref, nx_ref, nc_ref,
                  cw_ref, g_ref, ws_ref, bs_ref, conv_ref, cmlp_ref, *, tn):
    i = pl.program_id(0)
    last = pl.num_programs(0) - 1
    cx = cc_ref[...] * cx_ref[...]
    prev = (pc_ref[...] * px_ref[...])[SUBLANES - 1:SUBLANES, :]
    nxt = (nc_ref[...] * nx_ref[...])[0:1, :]
    prev = jnp.where(i == 0, 0.0, prev)
    nxt = jnp.where(i == last, 0.0, nxt)
    row = _iota((tn, 1), 0)
    up = jnp.where(row == 0, prev, pltpu.roll(cx, 1, 0))
    dn = jnp.where(row == tn - 1, nxt, pltpu.roll(cx, tn - 1, 0))
    w = cw_ref[...]
    conv_ref[...] = cb_ref[...] * (w[0:1] * up + w[1:2] * cx + w[2:3] * dn)

    v = v_ref[...]
    vn = v * lax.rsqrt(jnp.mean(v * v, axis=-1, keepdims=True) + EPS) * g_ref[...]
    head = _iota((1, GROUP_W), 1) >> 6
    for c in range(tn // CHUNK):
        sl = slice(c * CHUNK, (c + 1) * CHUNK)
        vc = vn[sl]
        acc = bs_ref[...]
        for h in range(GROUP_W // HEAD_DIM):
            vm = jnp.where(head == h, vc, 0.0).astype(BF16)
            acc = acc + jnp.dot(ws_ref[h], vm, preferred_element_type=F32)
        cmlp_ref[sl, :] = u_ref[sl, :] * acc


def _local_mixers(z, conv_w, cmlp_g, cmlp_ws, cmlp_bias):
    n = z.shape[0]
    tn = min(n, 256)
    tb = tn // SUBLANES
    nb8 = n // SUBLANES

    def col(j):
        return pl.BlockSpec((tn, GROUP_W), lambda i: (i, j))

    def prev(j):
        return pl.BlockSpec((SUBLANES, GROUP_W), lambda i: (jnp.maximum(i * tb - 1, 0), j))

    def nxt(j):
        return pl.BlockSpec((SUBLANES, GROUP_W), lambda i: (jnp.minimum((i + 1) * tb, nb8 - 1), j))

    out = pl.BlockSpec((tn, GROUP_W), lambda i: (i, 0))
    return pl.pallas_call(
        functools.partial(_local_kernel, tn=tn),
        grid=(n // tn,),
        in_specs=[col(0), col(1), col(2), col(3), col(4), prev(0), prev(2), nxt(0), nxt(2),
                  pl.BlockSpec((3, GROUP_W), lambda i: (0, 0)),
                  pl.BlockSpec((1, GROUP_W), lambda i: (0, 0)),
                  pl.BlockSpec((4, CHUNK, CHUNK), lambda i: (0, 0, 0)),
                  pl.BlockSpec((CHUNK, GROUP_W), lambda i: (0, 0))],
        out_specs=[out, out],
        out_shape=[jax.ShapeDtypeStruct((n, GROUP_W), F32)] * 2,
        compiler_params=_cparams(("arbitrary",)),
        name="local_mixers",
    )(z, z, z, z, z, z, z, z, z, conv_w, cmlp_g, cmlp_ws, cmlp_bias)


def _attprep_kernel(*refs, rope):
    if rope:
        q_ref, kv_ref, qg_ref, kg_ref, cos_ref, sin_ref, qo_ref, kt_ref, vd_ref = refs
    else:
        q_ref, kv_ref, qg_ref, kg_ref, qo_ref, kt_ref, vd_ref = refs
    q = q_ref[...]
    kv = kv_ref[...]
    k = kv[:, :LANES]
    v = kv[:, LANES:]
    same_head = (_iota((LANES, LANES), 0) >> 6) == (_iota((LANES, LANES), 1) >> 6)
    bd = jnp.where(same_head, 1.0 / HEAD_DIM, 0.0)
    lane = _iota((1, LANES), 1)
    first_half = (lane & 31) < 16

    def head_norm(x, g):
        ms = jnp.dot(x * x, bd, precision=HIGHEST, preferred_element_type=F32)
        return x * lax.rsqrt(ms + EPS) * g

    def rotary(x):
        if not rope:
            return x
        rot = jnp.where(first_half, -pltpu.roll(x, LANES - 16, 1), pltpu.roll(x, 16, 1))
        return x * cos_ref[...] + rot * sin_ref[...]

    for half in range(2):
        sl = slice(half * LANES, (half + 1) * LANES)
        qh = rotary(head_norm(q[:, sl], qg_ref[...]))
        qo_ref[:, sl] = (qh * (HEAD_DIM ** -0.5 * LOG2E)).astype(BF16)
    kn = rotary(head_norm(k, kg_ref[...]))
    kt = kn.T.astype(BF16)
    kt_ref[0:64, :] = kt[0:64]
    kt_ref[64:128, :] = kt[0:64]
    kt_ref[128:192, :] = kt[64:128]
    kt_ref[192:256, :] = kt[64:128]
    vr = pltpu.roll(v, HEAD_DIM, 1)
    low = lane < HEAD_DIM
    vd_ref[:, 0:LANES] = jnp.where(low, v, vr).astype(BF16)
    vd_ref[:, LANES:] = jnp.where(low, vr, v).astype(BF16)


def _attprep(z, q_g, k_g, tables):
    n = z.shape[0]
    tn = min(n, 256)
    rope = tables is not None
    in_specs = [pl.BlockSpec((tn, GROUP_W), lambda i: (i, 5)),
                pl.BlockSpec((tn, GROUP_W), lambda i: (i, 6)),
                pl.BlockSpec((1, LANES), lambda i: (0, 0)),
                pl.BlockSpec((1, LANES), lambda i: (0, 0))]
    args = [z, z, q_g, k_g]
    if rope:
        in_specs += [pl.BlockSpec((tn, LANES), lambda i: (i, 0))] * 2
        args += list(tables)
    return pl.pallas_call(
        functools.partial(_attprep_kernel, rope=rope),
        grid=(n // tn,),
        in_specs=in_specs,
        out_specs=[pl.BlockSpec((tn, GROUP_W), lambda i: (i, 0)),
                   pl.BlockSpec((GROUP_W, tn), lambda i: (0, i)),
                   pl.BlockSpec((tn, GROUP_W), lambda i: (i, 0))],
        out_shape=[jax.ShapeDtypeStruct((n, GROUP_W), BF16),
                   jax.ShapeDtypeStruct((GROUP_W, n), BF16),
                   jax.ShapeDtypeStruct((n, GROUP_W), BF16)],
        compiler_params=_cparams(("arbitrary",)),
        name="attprep_rope" if rope else "attprep",
    )(*args)


def _flash_kernel(q_ref, k_ref, v_ref, o_ref, s_ref, mx_ref, *, tk, nk):
    q = q_ref[...]
    tq = q.shape[0]
    low = _iota((1, LANES), 1) < HEAD_DIM
    zero = jnp.zeros_like(q)
    qa = jnp.where(low, q, zero)
    qb = jnp.where(low, zero, q)

    def scores(j, slot):
        k = k_ref[:, pl.ds(pl.multiple_of(j * tk, LANES), tk)]
        for h, qh in enumerate((qa, qb)):
            s = jnp.dot(qh, k, preferred_element_type=F32)
            s_ref[slot, h] = s
            mx_ref[slot, h] = jnp.max(s, axis=1, keepdims=True)

    def consume(j, slot, carry):
        m_a, acc_a, m_b, acc_b = carry
        v = v_ref[pl.ds(pl.multiple_of(j * tk, LANES), tk), :]
        ones = jnp.ones_like(v)
        n_a = jnp.maximum(m_a, mx_ref[slot, 0])
        n_b = jnp.maximum(m_b, mx_ref[slot, 1])
        p_a = jnp.exp2(s_ref[slot, 0] - n_a).astype(BF16)
        p_b = jnp.exp2(s_ref[slot, 1] - n_b).astype(BF16)
        acc_a = acc_a * jnp.exp2(m_a - n_a) + jnp.dot(p_a, jnp.where(low, v, ones), preferred_element_type=F32)
        acc_b = acc_b * jnp.exp2(m_b - n_b) + jnp.dot(p_b, jnp.where(low, ones, v), preferred_element_type=F32)
        return n_a, acc_a, n_b, acc_b

    neg = jnp.full((tq, 1), -jnp.inf, F32)
    za = jnp.zeros((tq, LANES), F32)
    carry = (neg, za, neg, za)
    scores(0, 0)
    if nk % 2 == 0 and nk >= 4:
        def pair(i, carry):
            j = 2 * i
            scores(j + 1, 1)
            carry = consume(j, 0, carry)
            scores(j + 2, 0)
            return consume(j + 1, 1, carry)

        carry = lax.fori_loop(0, nk // 2 - 1, pair, carry)
        scores(nk - 1, 1)
        carry = consume(nk - 2, 0, carry)
        carry = consume(nk - 1, 1, carry)
    else:
        for j in range(nk):
            if j + 1 < nk:
                scores(j + 1, (j + 1) % 2)
            carry = consume(j, j % 2, carry)
    _, acc_a, _, acc_b = carry
    o_ref[...] = jnp.where(low, acc_a / pltpu.roll(acc_a, HEAD_DIM, 1), acc_b / pltpu.roll(acc_b, HEAD_DIM, 1))


def _kv_tile(nk):
    best = LANES
    for t in range(LANES, FLASH_TK_MAX + 1, LANES):
        if nk % t == 0:
            best = t
    return best


def _flash(q, kt, vd):
    n = q.shape[0]
    nk = kt.shape[1]
    tq = min(n, FLASH_TQ)
    tk = _kv_tile(nk)
    return pl.pallas_call(
        functools.partial(_flash_kernel, tk=tk, nk=nk // tk),
        grid=(2, n // tq),
        in_specs=[pl.BlockSpec((tq, LANES), lambda g, i: (i, g)),
                  pl.BlockSpec((LANES, nk), lambda g, i: (g, 0)),
                  pl.BlockSpec((nk, LANES), lambda g, i: (0, g))],
        out_specs=pl.BlockSpec((tq, LANES), lambda g, i: (i, g)),
        out_shape=jax.ShapeDtypeStruct((n, GROUP_W), F32),
        scratch_shapes=[pltpu.VMEM((2, 2, tq, tk), F32), pltpu.VMEM((2, 2, tq, 1), F32)],
        compiler_params=_cparams(("arbitrary", "arbitrary")),
        name="flash_attention",
    )(q, kt, vd)


def _mlstm_kernel(*refs, fwd, merge):
    if merge:
        (q_ref, k_ref, v_ref, zg_ref, gb_ref, c0_ref, n0_ref, m0_ref, hf_ref, po_ref, ng_ref,
         h_ref, c_ref, n_ref, m_ref) = refs
    else:
        (q_ref, k_ref, v_ref, zg_ref, gb_ref, c0_ref, n0_ref, m0_ref,
         h_ref, c_ref, n_ref, m_ref) = refs

    @pl.when(pl.program_id(0) == 0)
    def _():
        c_ref[...] = c0_ref[...]
        n_ref[...] = n0_ref[...]
        m_ref[...] = m0_ref[...]

    off = 0 if fwd else 4
    nh = GROUP_W // HEAD_DIM
    q = q_ref[...]
    ks = k_ref[...] * (HEAD_DIM ** -0.5)
    v = v_ref[...]
    g = zg_ref[...] + gb_ref[...]
    lane = _iota((1, LANES), 1)
    lsig = jnp.minimum(g, 0.0) - jnp.log(1.0 + jnp.exp(-jnp.abs(g)))
    gcol = jnp.where((lane >= 8) & (lane < 16), lsig, g)
    lfcol = pltpu.roll(gcol, LANES - 8, 1)
    rr = _iota((CHUNK, CHUNK), 0)
    cc = _iota((CHUNK, CHUNK), 1)
    sees = (cc <= rr) if fwd else (cc >= rr)
    tri = jnp.where(sees, 1.0, 0.0)
    bc = jnp.dot(tri, lfcol, precision=HIGHEST, preferred_element_type=F32)
    bct = bc.T
    gt = gcol.T
    total = bc[CHUNK - 1:CHUNK, :] if fwd else bc[0:1, :]
    cp = c_ref[...]
    npv = n_ref[0:1, :]
    mp = m_ref[0:1, :]
    head = _iota((1, GROUP_W), 1) >> 6
    qb = q.astype(BF16)
    kb = ks.astype(BF16)
    qc = lax.dot_general(qb, cp.astype(BF16), NT_DIMS, preferred_element_type=F32)
    eh = jnp.where((_iota((GROUP_W, LANES), 0) >> 6) == _iota((GROUP_W, LANES), 1), 1.0, 0.0)
    nq = jnp.dot((q * npv).astype(BF16), eh.astype(BF16), preferred_element_type=F32)

    num = jnp.zeros((CHUNK, GROUP_W), F32)
    wi_x = jnp.zeros((CHUNK, GROUP_W), F32)
    den_x = jnp.ones((CHUNK, GROUP_W), F32)
    for h in range(nh):
        ch = off + h
        bc_t = bc[:, ch:ch + 1]
        dm = jnp.where(sees, bc_t - bct[ch:ch + 1, :] + gt[ch:ch + 1, :], -jnp.inf)
        inter = bc_t + mp[:, ch:ch + 1]
        m_t = jnp.maximum(inter, jnp.max(dm, axis=1, keepdims=True))
        hm = head == h
        qh = jnp.where(hm, q, 0.0).astype(BF16)
        s = lax.dot_general(qh, kb, NT_DIMS, preferred_element_type=F32) * jnp.exp(dm - m_t)
        wi = jnp.exp(inter - m_t)
        vh = jnp.where(hm, v, 0.0).astype(BF16)
        num = num + jnp.dot(s.astype(BF16), vh, preferred_element_type=F32)
        den = jnp.sum(s, axis=1, keepdims=True) + wi * nq[:, h:h + 1]
        dd = jnp.maximum(jnp.abs(den), jnp.exp(-m_t))
        wi_x = jnp.where(hm, wi, wi_x)
        den_x = jnp.where(hm, dd, den_x)
    hout = (num + wi_x * qc) / den_x

    a = total - bc + gcol
    m_loc = jnp.max(a, axis=0, keepdims=True)
    w = jnp.exp(a - m_loc)
    m_new = jnp.maximum(total + mp, m_loc)
    d_old = jnp.exp(total + mp - m_new)
    d_loc = jnp.exp(m_loc - m_new)
    w_x = jnp.zeros((CHUNK, GROUP_W), F32)
    dold_x = jnp.zeros((1, GROUP_W), F32)
    dloc_x = jnp.zeros((1, GROUP_W), F32)
    for h in range(nh):
        ch = off + h
        hm = head == h
        w_x = jnp.where(hm, w[:, ch:ch + 1], w_x)
        dold_x = jnp.where(hm, d_old[:, ch:ch + 1], dold_x)
        dloc_x = jnp.where(hm, d_loc[:, ch:ch + 1], dloc_x)
    c_loc = jnp.dot((w_x * v).T.astype(BF16), kb, preferred_element_type=F32)
    same_head = (_iota((GROUP_W, GROUP_W), 0) >> 6) == (_iota((GROUP_W, GROUP_W), 1) >> 6)
    c_ref[...] = cp * dold_x + jnp.where(same_head, c_loc, 0.0) * dloc_x
    n_new = npv * dold_x + jnp.sum(w_x * ks, axis=0, keepdims=True) * dloc_x
    n_ref[...] = jnp.broadcast_to(n_new, n_ref.shape)
    m_ref[...] = jnp.broadcast_to(m_new, m_ref.shape)

    if merge:
        hs = hf_ref[...] + hout
        bd = jnp.where(same_head, 1.0 / HEAD_DIM, 0.0)
        ms = jnp.dot(hs * hs, bd, precision=HIGHEST, preferred_element_type=F32)
        h_ref[...] = _sigmoid(po_ref[...]) * (hs * lax.rsqrt(ms + EPS) * ng_ref[...])
    else:
        h_ref[...] = hout


def _mlstm(z, gate_b, state, *, fwd, merge=None):
    n = z.shape[0]
    nc = n // CHUNK
    order = (lambda c: c) if fwd else (lambda c: nc - 1 - c)

    def col(j):
        return pl.BlockSpec((CHUNK, GROUP_W), lambda c: (order(c), j))

    def full(shape):
        return pl.BlockSpec(shape, lambda c: (0,) * len(shape))

    st_shapes = [(GROUP_W, GROUP_W), (SUBLANES, GROUP_W), (SUBLANES, LANES)]
    in_specs = [col(7), col(8), col(9),
                pl.BlockSpec((CHUNK, LANES), lambda c: (order(c), N_IN_BLOCKS - 1)),
                full((1, LANES))] + [full(s) for s in st_shapes]
    args = [z, z, z, z, gate_b, *state]
    if merge is not None:
        hf, norm_g = merge
        in_specs += [pl.BlockSpec((CHUNK, GROUP_W), lambda c: (order(c), 0)), col(10), full((1, GROUP_W))]
        args += [hf, z, norm_g]
    outs = pl.pallas_call(
        functools.partial(_mlstm_kernel, fwd=fwd, merge=merge is not None),
        grid=(nc,),
        in_specs=in_specs,
        out_specs=[pl.BlockSpec((CHUNK, GROUP_W), lambda c: (order(c), 0))] + [full(s) for s in st_shapes],
        out_shape=[jax.ShapeDtypeStruct((n, GROUP_W), F32)] + [jax.ShapeDtypeStruct(s, F32) for s in st_shapes],
        compiler_params=_cparams(("arbitrary",)),
        name="mlstm_fwd" if fwd else "mlstm_bwd",
    )(*args)
    return outs[0], tuple(outs[1:])


def _outproj_kernel(cv_ref, cm_ref, at_ref, ml_ref, w_ref, x_ref, g1_ref, ng_ref, sh_ref, sc_ref, rw_ref,
                    xo_ref, h_ref, aff_ref, *, n_exp):
    mix = jnp.zeros(x_ref.shape, F32)
    for j, ref in enumerate((cv_ref, cm_ref, at_ref, ml_ref)):
        mix = mix + jnp.dot(ref[...].astype(BF16), w_ref[j * GROUP_W:(j + 1) * GROUP_W, :],
                            preferred_element_type=F32)
    x = x_ref[...] + g1_ref[...] * mix
    xo_ref[...] = x
    ms = jnp.mean(x * x, axis=-1, keepdims=True)
    h = x * lax.rsqrt(ms + EPS) * ng_ref[...]
    h = h * (1.0 + sc_ref[...]) + sh_ref[...]
    h_ref[...] = h.astype(BF16)
    logits = jnp.dot(h, rw_ref[...], precision=HIGHEST, preferred_element_type=F32)
    lt = logits.T[0:n_exp, :]
    e = jnp.exp(lt - jnp.max(lt, axis=0, keepdims=True))
    aff_ref[...] = e / jnp.sum(e, axis=0, keepdims=True)


def _outproj(mixers, w_out, x, gate1, norm_g, shift, scale, router_pad, n_exp):
    n, d = x.shape
    tn = min(n, 512)
    mix_spec = pl.BlockSpec((tn, GROUP_W), lambda i: (i, 0))
    row = pl.BlockSpec((tn, d), lambda i: (i, 0))
    vec = pl.BlockSpec((1, d), lambda i: (0, 0))
    return pl.pallas_call(
        functools.partial(_outproj_kernel, n_exp=n_exp),
        grid=(n // tn,),
        in_specs=[mix_spec] * 4 + [pl.BlockSpec((d, d), lambda i: (0, 0)), row, vec, vec, vec, vec,
                                   pl.BlockSpec((d, LANES), lambda i: (0, 0))],
        out_specs=[row, row, pl.BlockSpec((n_exp, tn), lambda i: (0, i))],
        out_shape=[jax.ShapeDtypeStruct((n, d), F32), jax.ShapeDtypeStruct((n, d), BF16),
                   jax.ShapeDtypeStruct((n_exp, n), F32)],
        compiler_params=_cparams(("arbitrary",)),
        name="outproj_router",
    )(*mixers, w_out, x, gate1, norm_g, shift, scale, router_pad)


def _route_kernel(aff_ref, rank_ref, offs_ref, *, cap, iters):
    aff = aff_ref[...]
    n_exp = aff.shape[0]

    def count(mask):
        c = jnp.where(mask, 1.0, 0.0)
        return jnp.sum(jnp.sum(c, axis=1, keepdims=True), axis=2, keepdims=True)

    def bisect(_, carry):
        lo, hi = carry
        mid = 0.5 * (lo + hi)
        ok = count(aff >= mid) >= cap
        return jnp.where(ok, mid, lo), jnp.where(ok, hi, mid)

    _, hi = lax.fori_loop(0, iters, bisect,
                          (jnp.zeros((n_exp, 1, 1), F32), jnp.full((n_exp, 1, 1), 2.0, F32)))
    below = jnp.where(aff < hi, aff, -1.0)
    thr = jnp.max(jnp.max(below, axis=1, keepdims=True), axis=2, keepdims=True)
    gt = aff > thr
    eq = aff == thr
    need = cap - count(gt)

    rr = _iota((LANES, LANES), 0)
    cc = _iota((LANES, LANES), 1)
    upper = jnp.where(rr <= cc, 1.0, 0.0).astype(BF16)
    lower = jnp.where(cc < rr, 1.0, 0.0).astype(BF16)

    def prefix(x01):
        within = jnp.dot(x01.astype(BF16), upper, preferred_element_type=F32)
        offs = jnp.dot(lower, within.astype(BF16), preferred_element_type=F32)[:, LANES - 1:LANES]
        return within - x01 + offs, offs

    offs_all = jnp.zeros((ROUTE_BLOCKS, LANES), F32)
    for e in range(n_exp):
        gt_e = jnp.where(gt[e], 1.0, 0.0)
        eq_e = jnp.where(eq[e], 1.0, 0.0)
        eq_rank, _ = prefix(eq_e)
        sel = jnp.maximum(gt_e, jnp.where(eq_rank < need[e], eq_e, 0.0))
        rk, offs = prefix(sel)
        rk = jnp.where(sel > 0.0, rk, -1.0)
        rank_ref[e] = rk.astype(jnp.int32)
        taken = jnp.sum(jnp.sum(sel, axis=0, keepdims=True), axis=1, keepdims=True)
        offs_all = jnp.where(cc == e, offs, offs_all)
        offs_all = jnp.where(cc == n_exp + e, taken, offs_all)
    offs_ref[...] = offs_all.astype(jnp.int32)


def _route(aff_t, cap):
    n_exp, n = aff_t.shape
    assert n % LANES == 0 and n <= ROUTE_BLOCKS * LANES and cap <= n
    a = aff_t.reshape(n_exp, n // LANES, LANES)
    if n < ROUTE_BLOCKS * LANES:
        a = jnp.pad(a, ((0, 0), (0, ROUTE_BLOCKS - n // LANES), (0, 0)), constant_values=-1.0)
    blk = pl.BlockSpec((n_exp, ROUTE_BLOCKS, LANES), lambda i: (0, 0, 0))
    rank, offs = pl.pallas_call(
        functools.partial(_route_kernel, cap=cap, iters=ROUTE_ITERS),
        grid=(1,),
        in_specs=[blk],
        out_specs=[blk, pl.BlockSpec((ROUTE_BLOCKS, LANES), lambda i: (0, 0))],
        out_shape=[jax.ShapeDtypeStruct((n_exp, ROUTE_BLOCKS, LANES), jnp.int32),
                   jax.ShapeDtypeStruct((ROUTE_BLOCKS, LANES), jnp.int32)],
        compiler_params=_cparams(("arbitrary",)),
        name="route",
    )(a)
    offs_flat = jnp.concatenate([offs[:, :n_exp].T.reshape(-1), offs[0, n_exp:2 * n_exp]])
    return a, rank, offs_flat


def _gather_kernel(offs_ref, h_ref, rank_ref, gate_ref, xs_ref, gs_ref, idx_ref, acc_ref, gacc_ref, iacc_ref,
                   *, cap, spb):
    e = pl.program_id(0)
    sb = pl.program_id(1)

    @pl.when(sb == 0)
    def _():
        acc_ref[...] = jnp.zeros(acc_ref.shape, F32)
        gacc_ref[...] = jnp.zeros(gacc_ref.shape, F32)
        iacc_ref[...] = jnp.zeros(iacc_ref.shape, F32)

    rows = _iota((GATHER_WIN, LANES), 0)
    lane = _iota((1, LANES), 1)

    def body(k, carry):
        b = sb * spb + k
        r0 = offs_ref[e * ROUTE_BLOCKS + b]
        r0a = pl.multiple_of((r0 >> 3) << 3, SUBLANES)
        rrow = rank_ref[0, pl.ds(b, 1), :]
        grow = gate_ref[0, pl.ds(b, 1), :]
        trow = (b * LANES + lane).astype(F32)
        hit = (rrow - r0a) == rows
        hblk = h_ref[pl.ds(pl.multiple_of(k * LANES, LANES), LANES), :]
        onehot = jnp.where(hit, 1.0, 0.0).astype(BF16)
        rows_x = jnp.dot(onehot, hblk, preferred_element_type=F32)
        rows_g = jnp.sum(jnp.where(hit, grow, 0.0), axis=1, keepdims=True)
        rows_i = jnp.sum(jnp.where(hit, trow, 0.0), axis=1, keepdims=True)
        head = pl.ds(r0a, SUBLANES)
        tail = pl.ds(r0a + SUBLANES, LANES)
        for ref, val in ((acc_ref, rows_x), (gacc_ref, rows_g), (iacc_ref, rows_i)):
            ref[head, :] += val[0:SUBLANES]
            ref[tail, :] = val[SUBLANES:GATHER_WIN]
        return carry

    lax.fori_loop(0, spb, body, 0, unroll=min(spb, 4))

    @pl.when(sb == pl.num_programs(1) - 1)
    def _():
        xs_ref[0] = acc_ref[0:cap, :].astype(BF16)
        gs_ref[0] = gacc_ref[0:cap, :]
        idx_ref[0] = iacc_ref[0:cap, :].astype(jnp.int32)


def _gather(offs_flat, h2, rank, gate, cap):
    n, d = h2.shape
    n_exp = rank.shape[0]
    sbt = min(n, 2048)
    spb = sbt // LANES
    blk = pl.BlockSpec((1, ROUTE_BLOCKS, LANES), lambda e, s, offs: (e, 0, 0))
    return pl.pallas_call(
        functools.partial(_gather_kernel, cap=cap, spb=spb),
        grid_spec=pltpu.PrefetchScalarGridSpec(
            num_scalar_prefetch=1,
            grid=(n_exp, n // sbt),
            in_specs=[pl.BlockSpec((sbt, d), lambda e, s, offs: (s, 0)), blk, blk],
            out_specs=[pl.BlockSpec((1, cap, d), lambda e, s, offs: (e, 0, 0)),
                       pl.BlockSpec((1, cap, 1), lambda e, s, offs: (e, 0, 0)),
                       pl.BlockSpec((1, cap, 1), lambda e, s, offs: (e, 0, 0))],
            scratch_shapes=[pltpu.VMEM((cap + GATHER_WIN, d), F32),
                            pltpu.VMEM((cap + GATHER_WIN, 1), F32),
                            pltpu.VMEM((cap + GATHER_WIN, 1), F32)]),
        out_shape=[jax.ShapeDtypeStruct((n_exp, cap, d), BF16),
                   jax.ShapeDtypeStruct((n_exp, cap, 1), F32),
                   jax.ShapeDtypeStruct((n_exp, cap, 1), jnp.int32)],
        compiler_params=_cparams(("arbitrary", "arbitrary")),
        name="moe_gather",
    )(offs_flat, h2, rank, gate)


def _ffn_kernel(xs_ref, w1_ref, w3_ref, w2_ref, gs_ref, o_ref, *, cap, tm):
    f = pl.program_id(1)

    @pl.when(f == 0)
    def _():
        o_ref[...] = jnp.zeros(o_ref.shape, F32)

    w1 = w1_ref[0, 0].astype(BF16)
    w3 = w3_ref[0, 0].astype(BF16)
    w2 = w2_ref[0, 0].astype(BF16)
    for r in range(cap // tm):
        sl = slice(r * tm, (r + 1) * tm)
        x = xs_ref[0, sl, :]
        a = jnp.dot(x, w1, preferred_element_type=F32)
        b = jnp.dot(x, w3, preferred_element_type=F32)
        hid = (a * _sigmoid(a) * b).astype(BF16)
        o_ref[0, sl, :] += jnp.dot(hid, w2, preferred_element_type=F32)

    @pl.when(f == pl.num_programs(1) - 1)
    def _():
        o_ref[0] = o_ref[0] * gs_ref[0]


def _ffn(xs, gs, w1, w3, w2, layer):
    n_exp, cap, d = xs.shape
    ff = w1.shape[3]
    tf = 512
    tm = min(cap, 512)
    return pl.pallas_call(
        functools.partial(_ffn_kernel, cap=cap, tm=tm),
        grid=(n_exp, ff // tf),
        in_specs=[pl.BlockSpec((1, cap, d), lambda e, f: (e, 0, 0)),
                  pl.BlockSpec((1, 1, d, tf), lambda e, f: (layer, e, 0, f)),
                  pl.BlockSpec((1, 1, d, tf), lambda e, f: (layer, e, 0, f)),
                  pl.BlockSpec((1, 1, tf, d), lambda e, f: (layer, e, f, 0)),
                  pl.BlockSpec((1, cap, 1), lambda e, f: (e, 0, 0))],
        out_specs=pl.BlockSpec((1, cap, d), lambda e, f: (e, 0, 0)),
        out_shape=jax.ShapeDtypeStruct((n_exp, cap, d), F32),
        compiler_params=_cparams(("arbitrary", "arbitrary")),
        name="moe_ffn",
    )(xs, w1, w3, w2, gs)


def _combine_kernel(offs_ref, idx_ref, out_ref, y_ref, *, cap, tht):
    th = pl.program_id(0)
    e = pl.program_id(1)

    @pl.when(e == 0)
    def _():
        y_ref[...] = jnp.zeros(y_ref.shape, F32)

    bpt = tht // LANES
    first = e * ROUTE_BLOCKS + th * bpt
    start = offs_ref[first]
    is_last = th == pl.num_programs(0) - 1
    taken = offs_ref[pl.num_programs(1) * ROUTE_BLOCKS + e]
    end = jnp.where(is_last, taken, offs_ref[jnp.minimum(first + bpt, (e + 1) * ROUTE_BLOCKS - 1)])
    base = th * tht

    def token(r):
        return jnp.clip(idx_ref[e * cap + r] - base, 0, tht - 1)

    def group(i, carry):
        r = start + COMBINE_GROUP * i
        ts = [token(r + u) for u in range(COMBINE_GROUP)]
        sums = [y_ref[pl.ds(ts[u], 1), :] + out_ref[0, pl.ds(r + u, 1), :] for u in range(COMBINE_GROUP)]
        for u in range(COMBINE_GROUP):
            y_ref[pl.ds(ts[u], 1), :] = sums[u]
        return carry

    def single(r, carry):
        y_ref[pl.ds(token(r), 1), :] += out_ref[0, pl.ds(r, 1), :]
        return carry

    groups = lax.div(end - start, jnp.int32(COMBINE_GROUP))
    lax.fori_loop(0, groups, group, 0)
    lax.fori_loop(start + groups * COMBINE_GROUP, end, single, 0)


def _combine(offs_flat, idx_flat, out, n):
    n_exp, cap, d = out.shape
    tht = min(n, 4096)
    return pl.pallas_call(
        functools.partial(_combine_kernel, cap=cap, tht=tht),
        grid_spec=pltpu.PrefetchScalarGridSpec(
            num_scalar_prefetch=2,
            grid=(n // tht, n_exp),
            in_specs=[pl.BlockSpec((1, cap, d), lambda t, e, offs, idx: (e, 0, 0))],
            out_specs=pl.BlockSpec((tht, d), lambda t, e, offs, idx: (t, 0))),
        out_shape=jax.ShapeDtypeStruct((n, d), F32),
        compiler_params=_cparams(("arbitrary", "arbitrary")),
        name="moe_combine",
    )(offs_flat, idx_flat, out)


def _moe(h2, aff_t, w1, w3, w2, layer):
    n = h2.shape[0]
    n_exp = aff_t.shape[0]
    cap = CAPACITY_FACTOR * n // n_exp
    gate, rank, offs_flat = _route(aff_t, cap)
    xs, gs, idx = _gather(offs_flat, h2, rank, gate, cap)
    out = _ffn(xs, gs, w1, w3, w2, layer)
    return _combine(offs_flat, idx.reshape(-1), out, n)


def _final_kernel(x_ref, y_ref, g2_ref, ng_ref, o_ref):
    x = x_ref[...] + g2_ref[...] * y_ref[...]
    ms = jnp.mean(x * x, axis=-1, keepdims=True)
    o_ref[...] = x * lax.rsqrt(ms + EPS) * ng_ref[...]


def _final(x, y, g2, norm_g):
    n, d = x.shape
    tn = min(n, 512)
    row = pl.BlockSpec((tn, d), lambda i: (i, 0))
    vec = pl.BlockSpec((1, d), lambda i: (0, 0))
    return pl.pallas_call(
        _final_kernel,
        grid=(n // tn,),
        in_specs=[row, row, vec, vec],
        out_specs=row,
        out_shape=jax.ShapeDtypeStruct((n, d), F32),
        compiler_params=_cparams(("arbitrary",)),
        name="final_norm",
    )(x, y, g2, norm_g)


def _rope_tables(n):
    pos = jnp.arange(n)
    row, colp = pos // GRID_W, pos % GRID_W
    axis_dim = HEAD_DIM // 2
    inv_freq = ROPE_THETA ** (-jnp.arange(0, axis_dim, 2, dtype=F32) / axis_dim)

    def axis_angles(p):
        a = p.astype(F32)[:, None] * inv_freq[None, :]
        return jnp.concatenate([a, a], axis=-1)

    ang = jnp.concatenate([axis_angles(row), axis_angles(colp)], axis=-1)
    ang = jnp.concatenate([ang, ang], axis=-1)
    return jnp.cos(ang), jnp.sin(ang)


def _zero_state():
    return (jnp.zeros((GROUP_W, GROUP_W), F32), jnp.zeros((SUBLANES, GROUP_W), F32),
            jnp.zeros((SUBLANES, LANES), F32))


def kernel(x, c, ctx, c_ctx, ada_w, ada_b, norm1_g, w_in, conv_w, cmlp_norm_g, cmlp_ws, cmlp_bs, q_norm_g, k_norm_g, ml_igate_b, ml_fgate_b, ml_norm_g, w_out, norm2_g, router_w, exp_w1, exp_w3, exp_w2, final_norm_g):
    assert x.shape[0] == 1 and ctx.shape[0] == 1
    depth, d, d_in = w_in.shape
    n_exp = router_w.shape[-1]
    assert d_in <= N_IN_BLOCKS * LANES and d == 4 * GROUP_W
    xs = x[0]
    xc = ctx[0]
    n = xs.shape[0]

    mod = _modulation(jnp.stack([c[0], c_ctx], axis=1), ada_w, ada_b)
    tables = _rope_tables(n)
    row = lambda v: v.reshape(1, -1)

    res_x = None
    res_c = None
    for l in range(depth):
        parts = [[mod[l, s:s + 1, j * d:(j + 1) * d] for j in range(6)] for s in range(2)]
        sh1, sc1, g1, sh2, sc2, g2 = parts[0]
        csh1, csc1, cg1, csh2, csc2, cg2 = parts[1]
        w_in_l = jnp.pad(w_in[l], ((0, 0), (0, N_IN_BLOCKS * LANES - d_in))).astype(BF16)
        w_out_l = w_out[l].astype(BF16)
        cmlp_ws_l = cmlp_ws[l].astype(BF16)
        cmlp_bias = jnp.repeat(cmlp_bs[l].T, HEAD_DIM, axis=1)
        qg = jnp.tile(q_norm_g[l], 2).reshape(1, LANES)
        kg = jnp.tile(k_norm_g[l], 2).reshape(1, LANES)
        gate_b = jnp.pad(jnp.concatenate([ml_igate_b[l], ml_fgate_b[l]]), (0, LANES - 16)).reshape(1, LANES)
        router_pad = jnp.pad(router_w[l], ((0, 0), (0, LANES - n_exp)))
        update_ctx = l < depth - 1

        zc, xc = _inproj(xc, res_c, row(norm1_g[l]), csh1, csc1, w_in_l)
        qc_, ktc, vdc = _attprep(zc, qg, kg, None)
        hcf, st_f = _mlstm(zc, gate_b, _zero_state(), fwd=True)
        ml_c, st_b = _mlstm(zc, gate_b, _zero_state(), fwd=False, merge=(hcf, row(ml_norm_g[l])))
        if update_ctx:
            conv_c, cmlp_c = _local_mixers(zc, conv_w[l], row(cmlp_norm_g[l]), cmlp_ws_l, cmlp_bias)
            att_c = _flash(qc_, ktc, vdc)
            xc, hc2, affc = _outproj((conv_c, cmlp_c, att_c, ml_c), w_out_l, xc, cg1, row(norm2_g[l]),
                                     csh2, csc2, router_pad, n_exp)
            res_c = (_moe(hc2, affc, exp_w1, exp_w3, exp_w2, l), cg2)

        z, xs = _inproj(xs, res_x, row(norm1_g[l]), sh1, sc1, w_in_l)
        conv_x, cmlp_x = _local_mixers(z, conv_w[l], row(cmlp_norm_g[l]), cmlp_ws_l, cmlp_bias)
        q_, kt, vd = _attprep(z, qg, kg, tables)
        att_x = _flash(q_, jnp.concatenate([ktc, kt], axis=1), jnp.concatenate([vdc, vd], axis=0))
        hf, _ = _mlstm(z, gate_b, st_f, fwd=True)
        ml_x, _ = _mlstm(z, gate_b, st_b, fwd=False, merge=(hf, row(ml_norm_g[l])))
        xs, h2, aff = _outproj((conv_x, cmlp_x, att_x, ml_x), w_out_l, xs, g1, row(norm2_g[l]),
                               sh2, sc2, router_pad, n_exp)
        res_x = (_moe(h2, aff, exp_w1, exp_w3, exp_w2, l), g2)

    return _final(xs, res_x[0], res_x[1], row(final_norm_g))[None]
```

```python
import functools

import jax
import jax.numpy as jnp
from jax import lax
from jax.experimental import pallas as pl
from jax.experimental.pallas import tpu as pltpu

F32 = jnp.float32
BF16 = jnp.bfloat16
HIGHEST = lax.Precision.HIGHEST

EPS = 1e-6
LOG2E = 1.4426950408889634
GRID_W = 64
ROPE_THETA = 10000.0
CAPACITY_FACTOR = 2
GROUP_W = 256
HEAD_DIM = 64
CHUNK = 128
N_IN_BLOCKS = 23

LANES = 128
SUBLANES = 8
VMEM_LIMIT = 56 * 1024 * 1024

ROUTE_BLOCKS = 128
ROUTE_ITERS = 48
GATHER_WIN = 136
COMBINE_GROUP = 4
FFN_RIDER_ROWS = 64
FLASH_TQ = 256
FLASH_TK_MAX = 3328

NT_DIMS = (((1,), (1,)), ((), ()))


def _cparams(sem, vmem=None):
    return pltpu.CompilerParams(dimension_semantics=sem, vmem_limit_bytes=vmem or VMEM_LIMIT)


def _sigmoid(x):
    return 1.0 / (1.0 + jnp.exp(-x))


def _iota(shape, dim):
    return lax.broadcasted_iota(jnp.int32, shape, dim)


def _split(x, pieces):
    out = []
    for _ in range(pieces - 1):
        p = x.astype(BF16)
        out.append(p)
        x = x - p.astype(F32)
    return out + [x.astype(BF16)]


def _dot_f32_lhs(x, w):
    w = w.astype(BF16)
    return sum(jnp.dot(p, w, preferred_element_type=F32) for p in _split(x, 3))


def _dot_f32_rhs(w, x):
    w = w.astype(BF16)
    return sum(jnp.dot(w, p, preferred_element_type=F32) for p in _split(x, 3))


def _mod_kernel(cc_ref, w_ref, b_ref, o_ref):
    cc = cc_ref[...]
    s = cc * _sigmoid(cc)
    w = w_ref[0]
    b = b_ref[0]
    o_ref[0, 0:1, :] = jnp.sum(w * s[:, 0:1], axis=0, keepdims=True) + b
    o_ref[0, 1:2, :] = jnp.sum(w * s[:, 1:2], axis=0, keepdims=True) + b


def _modulation(cc, ada_w, ada_b):
    depth, d, m = ada_w.shape
    tn = 1536
    return pl.pallas_call(
        _mod_kernel,
        grid=(depth, m // tn),
        in_specs=[pl.BlockSpec((d, 2), lambda l, j: (0, 0)),
                  pl.BlockSpec((1, d, tn), lambda l, j: (l, 0, j)),
                  pl.BlockSpec((1, 1, tn), lambda l, j: (l, 0, j))],
        out_specs=pl.BlockSpec((1, 2, tn), lambda l, j: (l, 0, j)),
        out_shape=jax.ShapeDtypeStruct((depth, 2, m), F32),
        compiler_params=_cparams(("arbitrary", "arbitrary")),
        name="modulation",
    )(cc, ada_w, ada_b.reshape(depth, 1, m))


def _inproj_kernel(*refs, has_res):
    if has_res:
        x_ref, y_ref, g2_ref, ng_ref, sh_ref, sc_ref, w_ref, z_ref, xr_ref = refs
        x = x_ref[...] + g2_ref[...] * y_ref[...]
        xr_ref[...] = x
    else:
        x_ref, ng_ref, sh_ref, sc_ref, w_ref, z_ref = refs
        x = x_ref[...]
    ms = jnp.mean(x * x, axis=-1, keepdims=True)
    h = x * lax.rsqrt(ms + EPS) * ng_ref[...]
    h = h * (1.0 + sc_ref[...]) + sh_ref[...]
    z_ref[...] = jnp.dot(h.astype(BF16), w_ref[...], preferred_element_type=F32)


def _inproj(x, res, norm_g, shift, scale, w_in):
    n, d = x.shape
    dz = w_in.shape[1]
    tn = min(n, 512)
    row = pl.BlockSpec((tn, d), lambda i: (i, 0))
    vec = pl.BlockSpec((1, d), lambda i: (0, 0))
    wsp = pl.BlockSpec((d, dz), lambda i: (0, 0))
    zsp = pl.BlockSpec((tn, dz), lambda i: (i, 0))
    if res is None:
        z = pl.pallas_call(
            functools.partial(_inproj_kernel, has_res=False),
            grid=(n // tn,),
            in_specs=[row, vec, vec, vec, wsp],
            out_specs=zsp,
            out_shape=jax.ShapeDtypeStruct((n, dz), F32),
            compiler_params=_cparams(("arbitrary",)),
            name="inproj",
        )(x, norm_g, shift, scale, w_in)
        return z, x
    y, g2 = res
    z, xr = pl.pallas_call(
        functools.partial(_inproj_kernel, has_res=True),
        grid=(n // tn,),
        in_specs=[row, row, vec, vec, vec, vec, wsp],
        out_specs=[zsp, row],
        out_shape=[jax.ShapeDtypeStruct((n, dz), F32), jax.ShapeDtypeStruct((n, d), F32)],
        compiler_params=_cparams(("arbitrary",)),
        name="inproj_res",
    )(x, y, g2, norm_g, shift, scale, w_in)
    return z, xr


def _local_kernel(cx_ref, cb_ref, cc_ref, u_ref, v_ref, px_ref, pc_ref, nx_ref, nc_ref,
                  cw_ref, g_ref, ws_ref, bs_ref, conv_ref, cmlp_ref, *, tn):
    i = pl.program_id(0)
    last = pl.num_programs(0) - 1
    cx = cc_ref[...] * cx_ref[...]
    prev = (pc_ref[...] * px_ref[...])[SUBLANES - 1:SUBLANES, :]
    nxt = (nc_ref[...] * nx_ref[...])[0:1, :]
    prev = jnp.where(i == 0, 0.0, prev)
    nxt = jnp.where(i == last, 0.0, nxt)
    row = _iota((tn, 1), 0)
    up = jnp.where(row == 0, prev, pltpu.roll(cx, 1, 0))
    dn = jnp.where(row == tn - 1, nxt, pltpu.roll(cx, tn - 1, 0))
    w = cw_ref[...]
    conv_ref[...] = cb_ref[...] * (w[0:1] * up + w[1:2] * cx + w[2:3] * dn)

    v = v_ref[...]
    vn = v * lax.rsqrt(jnp.mean(v * v, axis=-1, keepdims=True) + EPS) * g_ref[...]
    head = _iota((1, GROUP_W), 1) >> 6
    for c in range(tn // CHUNK):
        sl = slice(c * CHUNK, (c + 1) * CHUNK)
        vc = vn[sl]
        acc = bs_ref[...]
        for h in range(GROUP_W // HEAD_DIM):
            vm = jnp.where(head == h, vc, 0.0).astype(BF16)
            acc = acc + jnp.dot(ws_ref[h], vm, preferred_element_type=F32)
        cmlp_ref[sl, :] = u_ref[sl, :] * acc


def _local_mixers(z, conv_w, cmlp_g, cmlp_ws, cmlp_bias):
    n = z.shape[0]
    tn = min(n, 256)
    tb = tn // SUBLANES
    nb8 = n // SUBLANES

    def col(j):
        return pl.BlockSpec((tn, GROUP_W), lambda i: (i, j))

    def prev(j):
        return pl.BlockSpec((SUBLANES, GROUP_W), lambda i: (jnp.maximum(i * tb - 1, 0), j))

    def nxt(j):
        return pl.BlockSpec((SUBLANES, GROUP_W), lambda i: (jnp.minimum((i + 1) * tb, nb8 - 1), j))

    out = pl.BlockSpec((tn, GROUP_W), lambda i: (i, 0))
    return pl.pallas_call(
        functools.partial(_local_kernel, tn=tn),
        grid=(n // tn,),
        in_specs=[col(0), col(1), col(2), col(3), col(4), prev(0), prev(2), nxt(0), nxt(2),
                  pl.BlockSpec((3, GROUP_W), lambda i: (0, 0)),
                  pl.BlockSpec((1, GROUP_W), lambda i: (0, 0)),
                  pl.BlockSpec((4, CHUNK, CHUNK), lambda i: (0, 0, 0)),
                  pl.BlockSpec((CHUNK, GROUP_W), lambda i: (0, 0))],
        out_specs=[out, out],
        out_shape=[jax.ShapeDtypeStruct((n, GROUP_W), F32)] * 2,
        compiler_params=_cparams(("arbitrary",)),
        name="local_mixers",
    )(z, z, z, z, z, z, z, z, z, conv_w, cmlp_g, cmlp_ws, cmlp_bias)


def _attprep_kernel(*refs, rope):
    if rope:
        q_ref, kv_ref, qg_ref, kg_ref, cos_ref, sin_ref, qo_ref, kt_ref, vd_ref = refs
    else:
        q_ref, kv_ref, qg_ref, kg_ref, qo_ref, kt_ref, vd_ref = refs
    q = q_ref[...]
    kv = kv_ref[...]
    k = kv[:, :LANES]
    v = kv[:, LANES:]
    same_head = (_iota((LANES, LANES), 0) >> 6) == (_iota((LANES, LANES), 1) >> 6)
    bd = jnp.where(same_head, 1.0 / HEAD_DIM, 0.0)
    lane = _iota((1, LANES), 1)
    first_half = (lane & 31) < 16

    def head_norm(x, g):
        ms = _dot_f32_lhs(x * x, bd)
        return x * lax.rsqrt(ms + EPS) * g

    def rotary(x):
        if not rope:
            return x
        rot = jnp.where(first_half, -pltpu.roll(x, LANES - 16, 1), pltpu.roll(x, 16, 1))
        return x * cos_ref[...] + rot * sin_ref[...]

    for half in range(2):
        sl = slice(half * LANES, (half + 1) * LANES)
        qh = rotary(head_norm(q[:, sl], qg_ref[...]))
        qo_ref[:, sl] = (qh * (HEAD_DIM ** -0.5 * LOG2E)).astype(BF16)
    kn = rotary(head_norm(k, kg_ref[...]))
    kt = kn.T.astype(BF16)
    kt_ref[0:64, :] = kt[0:64]
    kt_ref[64:128, :] = kt[0:64]
    kt_ref[128:192, :] = kt[64:128]
    kt_ref[192:256, :] = kt[64:128]
    vr = pltpu.roll(v, HEAD_DIM, 1)
    low = lane < HEAD_DIM
    vd_ref[:, 0:LANES] = jnp.where(low, v, vr).astype(BF16)
    vd_ref[:, LANES:] = jnp.where(low, vr, v).astype(BF16)


def _attprep(z, q_g, k_g, tables):
    n = z.shape[0]
    tn = min(n, 256)
    rope = tables is not None
    in_specs = [pl.BlockSpec((tn, GROUP_W), lambda i: (i, 5)),
                pl.BlockSpec((tn, GROUP_W), lambda i: (i, 6)),
                pl.BlockSpec((1, LANES), lambda i: (0, 0)),
                pl.BlockSpec((1, LANES), lambda i: (0, 0))]
    args = [z, z, q_g, k_g]
    if rope:
        in_specs += [pl.BlockSpec((tn, LANES), lambda i: (i, 0))] * 2
        args += list(tables)
    return pl.pallas_call(
        functools.partial(_attprep_kernel, rope=rope),
        grid=(n // tn,),
        in_specs=in_specs,
        out_specs=[pl.BlockSpec((tn, GROUP_W), lambda i: (i, 0)),
                   pl.BlockSpec((GROUP_W, tn), lambda i: (0, i)),
                   pl.BlockSpec((tn, GROUP_W), lambda i: (i, 0))],
        out_shape=[jax.ShapeDtypeStruct((n, GROUP_W), BF16),
                   jax.ShapeDtypeStruct((GROUP_W, n), BF16),
                   jax.ShapeDtypeStruct((n, GROUP_W), BF16)],
        compiler_params=_cparams(("arbitrary",)),
        name="attprep_rope" if rope else "attprep",
    )(*args)


def _flash_kernel(q_ref, k_ref, v_ref, o_ref, s_ref, mx_ref, *, tk, nk):
    q = q_ref[...]
    tq = q.shape[0]
    low = _iota((1, LANES), 1) < HEAD_DIM
    zero = jnp.zeros_like(q)
    qa = jnp.where(low, q, zero)
    qb = jnp.where(low, zero, q)

    def scores(j, slot):
        k = k_ref[:, pl.ds(pl.multiple_of(j * tk, LANES), tk)]
        for h, qh in enumerate((qa, qb)):
            s = jnp.dot(qh, k, preferred_element_type=F32)
            s_ref[slot, h] = s
            mx_ref[slot, h] = jnp.max(s, axis=1, keepdims=True)

    def consume(j, slot, carry):
        m_a, acc_a, m_b, acc_b = carry
        v = v_ref[pl.ds(pl.multiple_of(j * tk, LANES), tk), :]
        ones = jnp.ones_like(v)
        n_a = jnp.maximum(m_a, mx_ref[slot, 0])
        n_b = jnp.maximum(m_b, mx_ref[slot, 1])
        p_a = jnp.exp2(s_ref[slot, 0] - n_a).astype(BF16)
        p_b = jnp.exp2(s_ref[slot, 1] - n_b).astype(BF16)
        acc_a = acc_a * jnp.exp2(m_a - n_a) + jnp.dot(p_a, jnp.where(low, v, ones), preferred_element_type=F32)
        acc_b = acc_b * jnp.exp2(m_b - n_b) + jnp.dot(p_b, jnp.where(low, ones, v), preferred_element_type=F32)
        return n_a, acc_a, n_b, acc_b

    neg = jnp.full((tq, 1), -jnp.inf, F32)
    za = jnp.zeros((tq, LANES), F32)
    carry = (neg, za, neg, za)
    scores(0, 0)
    if nk % 2 == 0 and nk >= 4:
        def pair(i, carry):
            j = 2 * i
            scores(j + 1, 1)
            carry = consume(j, 0, carry)
            scores(j + 2, 0)
            return consume(j + 1, 1, carry)

        carry = lax.fori_loop(0, nk // 2 - 1, pair, carry)
        scores(nk - 1, 1)
        carry = consume(nk - 2, 0, carry)
        carry = consume(nk - 1, 1, carry)
    else:
        for j in range(nk):
            if j + 1 < nk:
                scores(j + 1, (j + 1) % 2)
            carry = consume(j, j % 2, carry)
    _, acc_a, _, acc_b = carry
    o_ref[...] = jnp.where(low, acc_a / pltpu.roll(acc_a, HEAD_DIM, 1), acc_b / pltpu.roll(acc_b, HEAD_DIM, 1))


def _kv_tile(nk):
    best = LANES
    for t in range(LANES, FLASH_TK_MAX + 1, LANES):
        if nk % t == 0:
            best = t
    return best


def _flash(q, kt, vd):
    n = q.shape[0]
    nk = kt.shape[1]
    tq = min(n, FLASH_TQ)
    tk = _kv_tile(nk)
    return pl.pallas_call(
        functools.partial(_flash_kernel, tk=tk, nk=nk // tk),
        grid=(2, n // tq),
        in_specs=[pl.BlockSpec((tq, LANES), lambda g, i: (i, g)),
                  pl.BlockSpec((LANES, nk), lambda g, i: (g, 0)),
                  pl.BlockSpec((nk, LANES), lambda g, i: (0, g))],
        out_specs=pl.BlockSpec((tq, LANES), lambda g, i: (i, g)),
        out_shape=jax.ShapeDtypeStruct((n, GROUP_W), F32),
        scratch_shapes=[pltpu.VMEM((2, 2, tq, tk), F32), pltpu.VMEM((2, 2, tq, 1), F32)],
        compiler_params=_cparams(("arbitrary", "arbitrary")),
        name="flash_attention",
    )(q, kt, vd)


def _mlstm_chunk(q, k, v, g, h_ref, c_ref, n_ref, m_ref, *, fwd):
    off = 0 if fwd else 4
    nh = GROUP_W // HEAD_DIM
    ks = k * (HEAD_DIM ** -0.5)
    lane = _iota((1, LANES), 1)
    lsig = jnp.minimum(g, 0.0) - jnp.log(1.0 + jnp.exp(-jnp.abs(g)))
    gcol = jnp.where((lane >= 8) & (lane < 16), lsig, g)
    lfcol = pltpu.roll(gcol, LANES - 8, 1)
    rr = _iota((CHUNK, CHUNK), 0)
    cc = _iota((CHUNK, CHUNK), 1)
    sees = (cc <= rr) if fwd else (cc >= rr)
    tri = jnp.where(sees, 1.0, 0.0)
    bc = _dot_f32_rhs(tri, lfcol)
    bct = bc.T
    gt = gcol.T
    total = bc[CHUNK - 1:CHUNK, :] if fwd else bc[0:1, :]
    cp = c_ref[...]
    npv = n_ref[0:1, :]
    mp = m_ref[0:1, :]
    head = _iota((1, GROUP_W), 1) >> 6
    qb = q.astype(BF16)
    kb = ks.astype(BF16)
    qc = lax.dot_general(qb, cp.astype(BF16), NT_DIMS, preferred_element_type=F32)
    heads = range(nh)
    stack = lambda parts: jnp.concatenate(parts, axis=0)
    restack = lambda x: stack([x[:, h * CHUNK:(h + 1) * CHUNK] for h in heads])
    pick = jnp.where(_iota((LANES, nh * CHUNK), 0) == off + (_iota((LANES, nh * CHUNK), 1) >> 7), 1.0, 0.0)
    eh = jnp.where((_iota((GROUP_W, nh * CHUNK), 0) >> 6) == (_iota((GROUP_W, nh * CHUNK), 1) >> 7), 1.0, 0.0)
    nq = restack(jnp.dot((q * npv).astype(BF16), eh.astype(BF16), preferred_element_type=F32))
    spread = _dot_f32_lhs(stack([bc, bc + mp]), pick)
    bc_t = restack(spread[0:CHUNK])
    inter = restack(spread[CHUNK:2 * CHUNK])
    row = stack([jnp.broadcast_to(gt[off + h:off + h + 1, :] - bct[off + h:off + h + 1, :],
                                  (CHUNK, CHUNK)) for h in heads])
    dm = jnp.where(stack([sees] * nh), bc_t + row, -jnp.inf)
    m_t = jnp.maximum(inter, jnp.max(dm, axis=1, keepdims=True))
    q_stack = stack([jnp.where(head == h, q, 0.0) for h in heads]).astype(BF16)
    v_stack = stack([jnp.where(head == h, v, 0.0) for h in heads]).astype(BF16)
    s = lax.dot_general(q_stack, kb, NT_DIMS, preferred_element_type=F32) * jnp.exp(dm - m_t)
    wi = jnp.exp(inter - m_t)
    s_lanes = jnp.concatenate([s[h * CHUNK:(h + 1) * CHUNK] for h in heads], axis=1).astype(BF16)
    num = jnp.dot(s_lanes, v_stack, preferred_element_type=F32)
    den = jnp.sum(s, axis=1, keepdims=True) + wi * nq
    dd = jnp.maximum(jnp.abs(den), jnp.exp(-m_t))
    wi_x = jnp.zeros((CHUNK, GROUP_W), F32)
    den_x = jnp.ones((CHUNK, GROUP_W), F32)
    for h in heads:
        rows = slice(h * CHUNK, (h + 1) * CHUNK)
        wi_x = jnp.where(head == h, jnp.concatenate([wi[rows]] * 2, axis=1), wi_x)
        den_x = jnp.where(head == h, jnp.concatenate([dd[rows]] * 2, axis=1), den_x)
    h_ref[...] = (num + wi_x * qc) / den_x

    a = total - bc + gcol
    m_loc = jnp.max(a, axis=0, keepdims=True)
    w = jnp.exp(a - m_loc)
    m_new = jnp.maximum(total + mp, m_loc)
    d_old = jnp.exp(total + mp - m_new)
    d_loc = jnp.exp(m_loc - m_new)
    pick_w = jnp.where(_iota((LANES, GROUP_W), 0) == off + (_iota((LANES, GROUP_W), 1) >> 6), 1.0, 0.0)
    w_x = _dot_f32_lhs(w, pick_w)
    decay = _dot_f32_lhs(stack([d_old, d_loc, jnp.zeros((SUBLANES - 2, LANES), F32)]), pick_w)
    dold_x = decay[0:1]
    dloc_x = decay[1:2]
    c_loc = jnp.dot((w_x * v).T.astype(BF16), kb, preferred_element_type=F32)
    same_head = (_iota((GROUP_W, GROUP_W), 0) >> 6) == (_iota((GROUP_W, GROUP_W), 1) >> 6)
    c_ref[...] = cp * dold_x + jnp.where(same_head, c_loc, 0.0) * dloc_x
    n_new = npv * dold_x + jnp.sum(w_x * ks, axis=0, keepdims=True) * dloc_x
    n_ref[...] = jnp.broadcast_to(n_new, n_ref.shape)
    m_ref[...] = jnp.broadcast_to(m_new, m_ref.shape)


def _mlstm_kernel(qf_ref, kf_ref, vf_ref, gf_ref, qb_ref, kb_ref, vb_ref, gb_ref, bias_ref,
                  c0f_ref, n0f_ref, m0f_ref, c0b_ref, n0b_ref, m0b_ref,
                  hf_ref, hb_ref, cf_ref, nf_ref, mf_ref, cb_ref, nb_ref, mb_ref):
    @pl.when(pl.program_id(0) == 0)
    def _():
        for dst, src in ((cf_ref, c0f_ref), (nf_ref, n0f_ref), (mf_ref, m0f_ref),
                         (cb_ref, c0b_ref), (nb_ref, n0b_ref), (mb_ref, m0b_ref)):
            dst[...] = src[...]

    _mlstm_chunk(qf_ref[...], kf_ref[...], vf_ref[...], gf_ref[...] + bias_ref[...],
                 hf_ref, cf_ref, nf_ref, mf_ref, fwd=True)
    _mlstm_chunk(qb_ref[...], kb_ref[...], vb_ref[...], gb_ref[...] + bias_ref[...],
                 hb_ref, cb_ref, nb_ref, mb_ref, fwd=False)


def _mlstm(z, gate_b, state_f, state_b):
    n = z.shape[0]
    nc = n // CHUNK

    def cols(order):
        return [pl.BlockSpec((CHUNK, GROUP_W), lambda c, j=j: (order(c), j)) for j in (7, 8, 9)] + [
            pl.BlockSpec((CHUNK, LANES), lambda c: (order(c), N_IN_BLOCKS - 1))]

    def full(shape):
        return pl.BlockSpec(shape, lambda c: (0,) * len(shape))

    fo = lambda c: c
    bo = lambda c: nc - 1 - c
    st_shapes = [(GROUP_W, GROUP_W), (SUBLANES, GROUP_W), (SUBLANES, LANES)]
    st_specs = [full(s) for s in st_shapes]
    outs = pl.pallas_call(
        _mlstm_kernel,
        grid=(nc,),
        in_specs=cols(fo) + cols(bo) + [full((1, LANES))] + st_specs * 2,
        out_specs=[pl.BlockSpec((CHUNK, GROUP_W), lambda c: (fo(c), 0)),
                   pl.BlockSpec((CHUNK, GROUP_W), lambda c: (bo(c), 0))] + st_specs * 2,
        out_shape=[jax.ShapeDtypeStruct((n, GROUP_W), F32)] * 2
        + [jax.ShapeDtypeStruct(s, F32) for s in st_shapes] * 2,
        compiler_params=_cparams(("arbitrary",)),
        name="mlstm",
    )(z, z, z, z, z, z, z, z, gate_b, *state_f, *state_b)
    return outs[0], outs[1], tuple(outs[2:5]), tuple(outs[5:8])


def _outproj_kernel(cv_ref, cm_ref, at_ref, hf_ref, hb_ref, po_ref, mg_ref, w_ref, x_ref, g1_ref, ng_ref,
                    sh_ref, sc_ref, rw_ref, xo_ref, h_ref, aff_ref, *, n_exp):
    hs = hf_ref[...] + hb_ref[...]
    same_head = (_iota((GROUP_W, GROUP_W), 0) >> 6) == (_iota((GROUP_W, GROUP_W), 1) >> 6)
    hms = _dot_f32_lhs(hs * hs, jnp.where(same_head, 1.0 / HEAD_DIM, 0.0))
    ml = _sigmoid(po_ref[...]) * (hs * lax.rsqrt(hms + EPS) * mg_ref[...])
    mix = jnp.zeros(x_ref.shape, F32)
    for j, val in enumerate((cv_ref[...], cm_ref[...], at_ref[...], ml)):
        mix = mix + jnp.dot(val.astype(BF16), w_ref[j * GROUP_W:(j + 1) * GROUP_W, :],
                            preferred_element_type=F32)
    x = x_ref[...] + g1_ref[...] * mix
    xo_ref[...] = x
    ms = jnp.mean(x * x, axis=-1, keepdims=True)
    h = x * lax.rsqrt(ms + EPS) * ng_ref[...]
    h = h * (1.0 + sc_ref[...]) + sh_ref[...]
    h_ref[...] = h.astype(BF16)
    h_hi, h_lo = _split(h, 2)
    w_hi, w_lo = _split(rw_ref[...], 2)
    logits = (jnp.dot(h_hi, w_hi, preferred_element_type=F32) + jnp.dot(h_lo, w_hi, preferred_element_type=F32)
              + jnp.dot(h_hi, w_lo, preferred_element_type=F32))
    lt = logits.T[0:n_exp, :]
    e = jnp.exp(lt - jnp.max(lt, axis=0, keepdims=True))
    aff_ref[...] = e / jnp.sum(e, axis=0, keepdims=True)


def _outproj(mixers, z, ml_norm_g, w_out, x, gate1, norm_g, shift, scale, router_pad, n_exp):
    n, d = x.shape
    tn = min(n, 512)
    mix_spec = pl.BlockSpec((tn, GROUP_W), lambda i: (i, 0))
    row = pl.BlockSpec((tn, d), lambda i: (i, 0))
    vec = pl.BlockSpec((1, d), lambda i: (0, 0))
    return pl.pallas_call(
        functools.partial(_outproj_kernel, n_exp=n_exp),
        grid=(n // tn,),
        in_specs=[mix_spec] * 5 + [pl.BlockSpec((tn, GROUP_W), lambda i: (i, 10)),
                                   pl.BlockSpec((1, GROUP_W), lambda i: (0, 0)),
                                   pl.BlockSpec((d, d), lambda i: (0, 0)), row, vec, vec, vec, vec,
                                   pl.BlockSpec((d, LANES), lambda i: (0, 0))],
        out_specs=[row, row, pl.BlockSpec((n_exp, tn), lambda i: (0, i))],
        out_shape=[jax.ShapeDtypeStruct((n, d), F32), jax.ShapeDtypeStruct((n, d), BF16),
                   jax.ShapeDtypeStruct((n_exp, n), F32)],
        compiler_params=_cparams(("arbitrary",)),
        name="outproj_router",
    )(*mixers, z, ml_norm_g, w_out, x, gate1, norm_g, shift, scale, router_pad)


def _route_kernel(aff_ref, rank_ref, offs_ref, *, cap, iters):
    aff = aff_ref[...]
    n_exp = aff.shape[0]

    def count(mask):
        c = jnp.where(mask, 1.0, 0.0)
        return jnp.sum(jnp.sum(c, axis=1, keepdims=True), axis=2, keepdims=True)

    def bisect(_, carry):
        lo, hi = carry
        mid = 0.5 * (lo + hi)
        ok = count(aff >= mid) >= cap
        return jnp.where(ok, mid, lo), jnp.where(ok, hi, mid)

    _, hi = lax.fori_loop(0, iters, bisect,
                          (jnp.zeros((n_exp, 1, 1), F32), jnp.full((n_exp, 1, 1), 2.0, F32)))
    below = jnp.where(aff < hi, aff, -1.0)
    thr = jnp.max(jnp.max(below, axis=1, keepdims=True), axis=2, keepdims=True)
    gt = aff > thr
    eq = aff == thr
    need = cap - count(gt)

    rr = _iota((LANES, LANES), 0)
    cc = _iota((LANES, LANES), 1)
    upper = jnp.where(rr <= cc, 1.0, 0.0).astype(BF16)
    lower = jnp.where(cc < rr, 1.0, 0.0).astype(BF16)

    def prefix(x01):
        within = jnp.dot(x01.astype(BF16), upper, preferred_element_type=F32)
        offs = jnp.dot(lower, within.astype(BF16), preferred_element_type=F32)[:, LANES - 1:LANES]
        return within - x01 + offs, offs

    offs_all = jnp.zeros((ROUTE_BLOCKS, LANES), F32)
    for e in range(n_exp):
        gt_e = jnp.where(gt[e], 1.0, 0.0)
        eq_e = jnp.where(eq[e], 1.0, 0.0)
        eq_rank, _ = prefix(eq_e)
        sel = jnp.maximum(gt_e, jnp.where(eq_rank < need[e], eq_e, 0.0))
        rk, offs = prefix(sel)
        rk = jnp.where(sel > 0.0, rk, -1.0)
        rank_ref[e] = rk.astype(jnp.int32)
        taken = jnp.sum(jnp.sum(sel, axis=0, keepdims=True), axis=1, keepdims=True)
        offs_all = jnp.where(cc == e, offs, offs_all)
        offs_all = jnp.where(cc == n_exp + e, taken, offs_all)
    offs_ref[...] = offs_all.astype(jnp.int32)


def _route(aff_t, cap):
    n_exp, n = aff_t.shape
    assert n % LANES == 0 and n <= ROUTE_BLOCKS * LANES and cap <= n
    a = aff_t.reshape(n_exp, n // LANES, LANES)
    if n < ROUTE_BLOCKS * LANES:
        a = jnp.pad(a, ((0, 0), (0, ROUTE_BLOCKS - n // LANES), (0, 0)), constant_values=-1.0)
    blk = pl.BlockSpec((n_exp, ROUTE_BLOCKS, LANES), lambda i: (0, 0, 0))
    rank, offs = pl.pallas_call(
        functools.partial(_route_kernel, cap=cap, iters=ROUTE_ITERS),
        grid=(1,),
        in_specs=[blk],
        out_specs=[blk, pl.BlockSpec((ROUTE_BLOCKS, LANES), lambda i: (0, 0))],
        out_shape=[jax.ShapeDtypeStruct((n_exp, ROUTE_BLOCKS, LANES), jnp.int32),
                   jax.ShapeDtypeStruct((ROUTE_BLOCKS, LANES), jnp.int32)],
        compiler_params=_cparams(("arbitrary",)),
        name="route",
    )(a)
    offs_flat = jnp.concatenate([offs[:, :n_exp].T.reshape(-1), offs[0, n_exp:2 * n_exp]])
    return a, rank, offs_flat


def _gather_kernel(offs_ref, h_ref, rank_ref, gate_ref, xs_ref, gs_ref, idx_ref, acc_ref, gacc_ref, iacc_ref,
                   *, cap, spb):
    e = pl.program_id(0)
    sb = pl.program_id(1)

    @pl.when(sb == 0)
    def _():
        acc_ref[...] = jnp.zeros(acc_ref.shape, F32)
        gacc_ref[...] = jnp.zeros(gacc_ref.shape, F32)
        iacc_ref[...] = jnp.zeros(iacc_ref.shape, F32)

    rows = _iota((GATHER_WIN, LANES), 0)
    lane = _iota((1, LANES), 1)

    def body(k, carry):
        b = sb * spb + k
        r0 = offs_ref[e * ROUTE_BLOCKS + b]
        r0a = pl.multiple_of((r0 >> 3) << 3, SUBLANES)
        rrow = rank_ref[0, pl.ds(b, 1), :]
        grow = gate_ref[0, pl.ds(b, 1), :]
        trow = (b * LANES + lane).astype(F32)
        hit = (rrow - r0a) == rows
        hblk = h_ref[pl.ds(pl.multiple_of(k * LANES, LANES), LANES), :]
        onehot = jnp.where(hit, 1.0, 0.0).astype(BF16)
        rows_x = jnp.dot(onehot, hblk, preferred_element_type=F32)
        rows_g = jnp.sum(jnp.where(hit, grow, 0.0), axis=1, keepdims=True)
        rows_i = jnp.sum(jnp.where(hit, trow, 0.0), axis=1, keepdims=True)
        head = pl.ds(r0a, SUBLANES)
        tail = pl.ds(r0a + SUBLANES, LANES)
        for ref, val in ((acc_ref, rows_x), (gacc_ref, rows_g), (iacc_ref, rows_i)):
            ref[head, :] += val[0:SUBLANES]
            ref[tail, :] = val[SUBLANES:GATHER_WIN]
        return carry

    lax.fori_loop(0, spb, body, 0, unroll=min(spb, 4))

    @pl.when(sb == pl.num_programs(1) - 1)
    def _():
        xs_ref[0] = acc_ref[0:cap, :].astype(BF16)
        gs_ref[0] = gacc_ref[0:cap, :]
        idx_ref[0] = iacc_ref[0:cap, :].astype(jnp.int32)


def _gather(offs_flat, h2, rank, gate, cap):
    n, d = h2.shape
    n_exp = rank.shape[0]
    sbt = min(n, 2048)
    spb = sbt // LANES
    blk = pl.BlockSpec((1, ROUTE_BLOCKS, LANES), lambda e, s, offs: (e, 0, 0))
    return pl.pallas_call(
        functools.partial(_gather_kernel, cap=cap, spb=spb),
        grid_spec=pltpu.PrefetchScalarGridSpec(
            num_scalar_prefetch=1,
            grid=(n_exp, n // sbt),
            in_specs=[pl.BlockSpec((sbt, d), lambda e, s, offs: (s, 0)), blk, blk],
            out_specs=[pl.BlockSpec((1, cap, d), lambda e, s, offs: (e, 0, 0)),
                       pl.BlockSpec((1, cap, 1), lambda e, s, offs: (e, 0, 0)),
                       pl.BlockSpec((1, cap, 1), lambda e, s, offs: (e, 0, 0))],
            scratch_shapes=[pltpu.VMEM((cap + GATHER_WIN, d), F32),
                            pltpu.VMEM((cap + GATHER_WIN, 1), F32),
                            pltpu.VMEM((cap + GATHER_WIN, 1), F32)]),
        out_shape=[jax.ShapeDtypeStruct((n_exp, cap, d), BF16),
                   jax.ShapeDtypeStruct((n_exp, cap, 1), F32),
                   jax.ShapeDtypeStruct((n_exp, cap, 1), jnp.int32)],
        compiler_params=_cparams(("arbitrary", "arbitrary")),
        name="moe_gather",
    )(offs_flat, h2, rank, gate)


def _ffn_tiles(caps, tm):
    tiles = [[(0, a, min(tm, caps[0] - a))] for a in range(0, caps[0], tm)]
    for s in range(1, len(caps)):
        if caps[s] <= FFN_RIDER_ROWS:
            tiles[-1].append((s, 0, caps[s]))
        else:
            tiles += [[(s, a, min(tm, caps[s] - a))] for a in range(0, caps[s], tm)]
    return tiles


def _ffn_kernel(*refs, caps, tm):
    ns = len(caps)
    xs_refs, (w1_ref, w3_ref, w2_ref) = refs[:ns], refs[ns:ns + 3]
    gs_refs, o_refs = refs[ns + 3:2 * ns + 3], refs[2 * ns + 3:]
    f = pl.program_id(1)

    @pl.when(f == 0)
    def _():
        for o_ref in o_refs:
            o_ref[...] = jnp.zeros(o_ref.shape, F32)

    w1 = w1_ref[0, 0].astype(BF16)
    w3 = w3_ref[0, 0].astype(BF16)
    w2 = w2_ref[0, 0].astype(BF16)
    for tile in _ffn_tiles(caps, tm):
        parts = [xs_refs[s][0, a:a + n, :] for s, a, n in tile]
        x = parts[0] if len(parts) == 1 else jnp.concatenate(parts, axis=0)
        a_ = jnp.dot(x, w1, preferred_element_type=F32)
        b_ = jnp.dot(x, w3, preferred_element_type=F32)
        hid = (a_ * _sigmoid(a_) * b_).astype(BF16)
        y = jnp.dot(hid, w2, preferred_element_type=F32)
        r = 0
        for s, a, n in tile:
            o_refs[s][0, a:a + n, :] += y[r:r + n]
            r += n

    @pl.when(f == pl.num_programs(1) - 1)
    def _():
        for o_ref, gs_ref in zip(o_refs, gs_refs):
            o_ref[0] = o_ref[0] * gs_ref[0]


def _ffn(row_sets, w1, w3, w2, layer):
    caps = tuple(xs.shape[1] for xs, _ in row_sets)
    n_exp, _, d = row_sets[0][0].shape
    ff = w1.shape[3]
    tf = 512
    tm = min(caps[0], 512)
    rows = lambda cap, w: pl.BlockSpec((1, cap, w), lambda e, f: (e, 0, 0))
    outs = pl.pallas_call(
        functools.partial(_ffn_kernel, caps=caps, tm=tm),
        grid=(n_exp, ff // tf),
        in_specs=[rows(c, d) for c in caps]
        + [pl.BlockSpec((1, 1, d, tf), lambda e, f: (layer, e, 0, f)),
           pl.BlockSpec((1, 1, d, tf), lambda e, f: (layer, e, 0, f)),
           pl.BlockSpec((1, 1, tf, d), lambda e, f: (layer, e, f, 0))]
        + [rows(c, 1) for c in caps],
        out_specs=[rows(c, d) for c in caps],
        out_shape=[jax.ShapeDtypeStruct((n_exp, c, d), F32) for c in caps],
        compiler_params=_cparams(("arbitrary", "arbitrary")),
        name="moe_ffn",
    )(*[xs for xs, _ in row_sets], w1, w3, w2, *[gs for _, gs in row_sets])
    return outs


def _combine_kernel(offs_ref, idx_ref, out_ref, y_ref, *, cap, tht):
    th = pl.program_id(0)
    e = pl.program_id(1)

    @pl.when(e == 0)
    def _():
        y_ref[...] = jnp.zeros(y_ref.shape, F32)

    bpt = tht // LANES
    first = e * ROUTE_BLOCKS + th * bpt
    start = offs_ref[first]
    is_last = th == pl.num_programs(0) - 1
    taken = offs_ref[pl.num_programs(1) * ROUTE_BLOCKS + e]
    end = jnp.where(is_last, taken, offs_ref[jnp.minimum(first + bpt, (e + 1) * ROUTE_BLOCKS - 1)])
    base = th * tht

    def token(r):
        return jnp.clip(idx_ref[e * cap + r] - base, 0, tht - 1)

    def group(i, carry):
        r = start + COMBINE_GROUP * i
        ts = [token(r + u) for u in range(COMBINE_GROUP)]
        sums = [y_ref[pl.ds(ts[u], 1), :] + out_ref[0, pl.ds(r + u, 1), :] for u in range(COMBINE_GROUP)]
        for u in range(COMBINE_GROUP):
            y_ref[pl.ds(ts[u], 1), :] = sums[u]
        return carry

    def single(r, carry):
        y_ref[pl.ds(token(r), 1), :] += out_ref[0, pl.ds(r, 1), :]
        return carry

    groups = lax.div(end - start, jnp.int32(COMBINE_GROUP))
    lax.fori_loop(0, groups, group, 0)
    lax.fori_loop(start + groups * COMBINE_GROUP, end, single, 0)


def _combine(offs_flat, idx_flat, out, n):
    n_exp, cap, d = out.shape
    tht = min(n, 4096)
    return pl.pallas_call(
        functools.partial(_combine_kernel, cap=cap, tht=tht),
        grid_spec=pltpu.PrefetchScalarGridSpec(
            num_scalar_prefetch=2,
            grid=(n // tht, n_exp),
            in_specs=[pl.BlockSpec((1, cap, d), lambda t, e, offs, idx: (e, 0, 0))],
            out_specs=pl.BlockSpec((tht, d), lambda t, e, offs, idx: (t, 0))),
        out_shape=jax.ShapeDtypeStruct((n, d), F32),
        compiler_params=_cparams(("arbitrary", "arbitrary")),
        name="moe_combine",
    )(offs_flat, idx_flat, out)


def _moe(routed_sets, w1, w3, w2, layer):
    plans = []
    for h2, aff_t in routed_sets:
        n = h2.shape[0]
        cap = CAPACITY_FACTOR * n // aff_t.shape[0]
        gate, rank, offs_flat = _route(aff_t, cap)
        xs, gs, idx = _gather(offs_flat, h2, rank, gate, cap)
        plans.append((offs_flat, idx.reshape(-1), xs, gs, n))
    outs = _ffn([(p[2], p[3]) for p in plans], w1, w3, w2, layer)
    return [_combine(p[0], p[1], out, p[4]) for p, out in zip(plans, outs)]


def _final_kernel(x_ref, y_ref, g2_ref, ng_ref, o_ref):
    x = x_ref[...] + g2_ref[...] * y_ref[...]
    ms = jnp.mean(x * x, axis=-1, keepdims=True)
    o_ref[...] = x * lax.rsqrt(ms + EPS) * ng_ref[...]


def _final(x, y, g2, norm_g):
    n, d = x.shape
    tn = min(n, 512)
    row = pl.BlockSpec((tn, d), lambda i: (i, 0))
    vec = pl.BlockSpec((1, d), lambda i: (0, 0))
    return pl.pallas_call(
        _final_kernel,
        grid=(n // tn,),
        in_specs=[row, row, vec, vec],
        out_specs=row,
        out_shape=jax.ShapeDtypeStruct((n, d), F32),
        compiler_params=_cparams(("arbitrary",)),
        name="final_norm",
    )(x, y, g2, norm_g)


def _rope_tables(n):
    pos = jnp.arange(n)
    row, colp = pos // GRID_W, pos % GRID_W
    axis_dim = HEAD_DIM // 2
    inv_freq = ROPE_THETA ** (-jnp.arange(0, axis_dim, 2, dtype=F32) / axis_dim)

    def axis_angles(p):
        a = p.astype(F32)[:, None] * inv_freq[None, :]
        return jnp.concatenate([a, a], axis=-1)

    ang = jnp.concatenate([axis_angles(row), axis_angles(colp)], axis=-1)
    ang = jnp.concatenate([ang, ang], axis=-1)
    return jnp.cos(ang), jnp.sin(ang)


def _zero_state():
    return (jnp.zeros((GROUP_W, GROUP_W), F32), jnp.zeros((SUBLANES, GROUP_W), F32),
            jnp.zeros((SUBLANES, LANES), F32))


def kernel(x, c, ctx, c_ctx, ada_w, ada_b, norm1_g, w_in, conv_w, cmlp_norm_g, cmlp_ws, cmlp_bs, q_norm_g, k_norm_g, ml_igate_b, ml_fgate_b, ml_norm_g, w_out, norm2_g, router_w, exp_w1, exp_w3, exp_w2, final_norm_g):
    assert x.shape[0] == 1 and ctx.shape[0] == 1
    depth, d, d_in = w_in.shape
    n_exp = router_w.shape[-1]
    assert d_in <= N_IN_BLOCKS * LANES and d == 4 * GROUP_W
    xs = x[0]
    xc = ctx[0]
    n = xs.shape[0]

    mod = _modulation(jnp.stack([c[0], c_ctx], axis=1), ada_w, ada_b)
    tables = _rope_tables(n)
    row = lambda v: v.reshape(1, -1)

    res_x = None
    res_c = None
    for l in range(depth):
        parts = [[mod[l, s:s + 1, j * d:(j + 1) * d] for j in range(6)] for s in range(2)]
        sh1, sc1, g1, sh2, sc2, g2 = parts[0]
        csh1, csc1, cg1, csh2, csc2, cg2 = parts[1]
        w_in_l = jnp.pad(w_in[l], ((0, 0), (0, N_IN_BLOCKS * LANES - d_in))).astype(BF16)
        w_out_l = w_out[l].astype(BF16)
        cmlp_ws_l = cmlp_ws[l].astype(BF16)
        cmlp_bias = jnp.repeat(cmlp_bs[l].T, HEAD_DIM, axis=1)
        qg = jnp.tile(q_norm_g[l], 2).reshape(1, LANES)
        kg = jnp.tile(k_norm_g[l], 2).reshape(1, LANES)
        gate_b = jnp.pad(jnp.concatenate([ml_igate_b[l], ml_fgate_b[l]]), (0, LANES - 16)).reshape(1, LANES)
        router_pad = jnp.pad(router_w[l], ((0, 0), (0, LANES - n_exp)))
        update_ctx = l < depth - 1

        zc, xc = _inproj(xc, res_c, row(norm1_g[l]), csh1, csc1, w_in_l)
        qc_, ktc, vdc = _attprep(zc, qg, kg, None)
        hcf, hcb, st_f, st_b = _mlstm(zc, gate_b, _zero_state(), _zero_state())
        if update_ctx:
            conv_c, cmlp_c = _local_mixers(zc, conv_w[l], row(cmlp_norm_g[l]), cmlp_ws_l, cmlp_bias)
            att_c = _flash(qc_, ktc, vdc)
            xc, hc2, affc = _outproj((conv_c, cmlp_c, att_c, hcf, hcb), zc, row(ml_norm_g[l]), w_out_l, xc, cg1,
                                     row(norm2_g[l]), csh2, csc2, router_pad, n_exp)

        z, xs = _inproj(xs, res_x, row(norm1_g[l]), sh1, sc1, w_in_l)
        conv_x, cmlp_x = _local_mixers(z, conv_w[l], row(cmlp_norm_g[l]), cmlp_ws_l, cmlp_bias)
        q_, kt, vd = _attprep(z, qg, kg, tables)
        att_x = _flash(q_, jnp.concatenate([ktc, kt], axis=1), jnp.concatenate([vdc, vd], axis=0))
        hf, hb, _, _ = _mlstm(z, gate_b, st_f, st_b)
        xs, h2, aff = _outproj((conv_x, cmlp_x, att_x, hf, hb), z, row(ml_norm_g[l]), w_out_l, xs, g1,
                               row(norm2_g[l]), sh2, sc2, router_pad, n_exp)
        routed = [(h2, aff)] + ([(hc2, affc)] if update_ctx else [])
        ys = _moe(routed, exp_w1, exp_w3, exp_w2, l)
        res_x = (ys[0], g2)
        if update_ctx:
            res_c = (ys[1], cg2)

    return _final(xs, res_x[0], res_x[1], row(final_norm_g))[None]
```

```python
import functools

import jax
import jax.numpy as jnp
from jax import lax
from jax.experimental import pallas as pl
from jax.experimental.pallas import tpu as pltpu

F32 = jnp.float32
BF16 = jnp.bfloat16
HIGHEST = lax.Precision.HIGHEST

EPS = 1e-6
LOG2E = 1.4426950408889634
GRID_W = 64
ROPE_THETA = 10000.0
CAPACITY_FACTOR = 2
GROUP_W = 256
HEAD_DIM = 64
CHUNK = 128
N_IN_BLOCKS = 23

LANES = 128
SUBLANES = 8
VMEM_LIMIT = 56 * 1024 * 1024

ROUTE_BLOCKS = 128
ROUTE_ITERS = 48
GATHER_WIN = 136
GATHER_WIN_SHORT = 40
GATHER_GROUP = 4
COMBINE_GROUP = 4
FFN_RIDER_ROWS = 64
FLASH_TQ = 512
FLASH_TK_MAX = 3328

NT_DIMS = (((1,), (1,)), ((), ()))


def _cparams(sem, vmem=None):
    return pltpu.CompilerParams(dimension_semantics=sem, vmem_limit_bytes=vmem or VMEM_LIMIT)


def _sigmoid(x):
    return 1.0 / (1.0 + jnp.exp(-x))


def _iota(shape, dim):
    return lax.broadcasted_iota(jnp.int32, shape, dim)


def _split(x, pieces):
    out = []
    for _ in range(pieces - 1):
        p = x.astype(BF16)
        out.append(p)
        x = x - p.astype(F32)
    return out + [x.astype(BF16)]


def _dot_f32_lhs(x, w):
    w = w.astype(BF16)
    return sum(jnp.dot(p, w, preferred_element_type=F32) for p in _split(x, 3))


def _dot_f32_rhs(w, x):
    w = w.astype(BF16)
    return sum(jnp.dot(w, p, preferred_element_type=F32) for p in _split(x, 3))


def _mod_kernel(cc_ref, w_ref, b_ref, o_ref):
    cc = cc_ref[...]
    s = cc * _sigmoid(cc)
    w = w_ref[0]
    b = b_ref[0]
    o_ref[0, 0:1, :] = jnp.sum(w * s[:, 0:1], axis=0, keepdims=True) + b
    o_ref[0, 1:2, :] = jnp.sum(w * s[:, 1:2], axis=0, keepdims=True) + b


def _modulation(cc, ada_w, ada_b):
    depth, d, m = ada_w.shape
    tn = 1536
    return pl.pallas_call(
        _mod_kernel,
        grid=(depth, m // tn),
        in_specs=[pl.BlockSpec((d, 2), lambda l, j: (0, 0)),
                  pl.BlockSpec((1, d, tn), lambda l, j: (l, 0, j)),
                  pl.BlockSpec((1, 1, tn), lambda l, j: (l, 0, j))],
        out_specs=pl.BlockSpec((1, 2, tn), lambda l, j: (l, 0, j)),
        out_shape=jax.ShapeDtypeStruct((depth, 2, m), F32),
        compiler_params=_cparams(("arbitrary", "arbitrary")),
        name="modulation",
    )(cc, ada_w, ada_b.reshape(depth, 1, m))


def _inproj_kernel(*refs, has_res):
    if has_res:
        x_ref, y_ref, g2_ref, ng_ref, sh_ref, sc_ref, w_ref, z_ref, xr_ref = refs
        x = x_ref[...] + g2_ref[...] * y_ref[...]
        xr_ref[...] = x
    else:
        x_ref, ng_ref, sh_ref, sc_ref, w_ref, z_ref = refs
        x = x_ref[...]
    ms = jnp.mean(x * x, axis=-1, keepdims=True)
    h = x * lax.rsqrt(ms + EPS) * ng_ref[...]
    h = h * (1.0 + sc_ref[...]) + sh_ref[...]
    z_ref[...] = jnp.dot(h.astype(BF16), w_ref[...], preferred_element_type=F32)


def _inproj(x, res, norm_g, shift, scale, w_in):
    n, d = x.shape
    dz = w_in.shape[1]
    tn = min(n, 512)
    row = pl.BlockSpec((tn, d), lambda i: (i, 0))
    vec = pl.BlockSpec((1, d), lambda i: (0, 0))
    wsp = pl.BlockSpec((d, dz), lambda i: (0, 0))
    zsp = pl.BlockSpec((tn, dz), lambda i: (i, 0))
    if res is None:
        z = pl.pallas_call(
            functools.partial(_inproj_kernel, has_res=False),
            grid=(n // tn,),
            in_specs=[row, vec, vec, vec, wsp],
            out_specs=zsp,
            out_shape=jax.ShapeDtypeStruct((n, dz), F32),
            compiler_params=_cparams(("arbitrary",)),
            name="inproj",
        )(x, norm_g, shift, scale, w_in)
        return z, x
    y, g2 = res
    z, xr = pl.pallas_call(
        functools.partial(_inproj_kernel, has_res=True),
        grid=(n // tn,),
        in_specs=[row, row, vec, vec, vec, vec, wsp],
        out_specs=[zsp, row],
        out_shape=[jax.ShapeDtypeStruct((n, dz), F32), jax.ShapeDtypeStruct((n, d), F32)],
        compiler_params=_cparams(("arbitrary",)),
        name="inproj_res",
    )(x, y, g2, norm_g, shift, scale, w_in)
    return z, xr


def _local_kernel(cx_ref, cb_ref, cc_ref, u_ref, v_ref, px_ref, pc_ref, nx_ref, nc_ref,
                  cw_ref, g_ref, ws_ref, bs_ref, conv_ref, cmlp_ref, *, tn):
    i = pl.program_id(0)
    last = pl.num_programs(0) - 1
    cx = cc_ref[...] * cx_ref[...]
    prev = (pc_ref[...] * px_ref[...])[SUBLANES - 1:SUBLANES, :]
    nxt = (nc_ref[...] * nx_ref[...])[0:1, :]
    prev = jnp.where(i == 0, 0.0, prev)
    nxt = jnp.where(i == last, 0.0, nxt)
    row = _iota((tn, 1), 0)
    up = jnp.where(row == 0, prev, pltpu.roll(cx, 1, 0))
    dn = jnp.where(row == tn - 1, nxt, pltpu.roll(cx, tn - 1, 0))
    w = cw_ref[...]
    conv_ref[...] = cb_ref[...] * (w[0:1] * up + w[1:2] * cx + w[2:3] * dn)

    v = v_ref[...]
    vn = v * lax.rsqrt(jnp.mean(v * v, axis=-1, keepdims=True) + EPS) * g_ref[...]
    head = _iota((1, GROUP_W), 1) >> 6
    for c in range(tn // CHUNK):
        sl = slice(c * CHUNK, (c + 1) * CHUNK)
        vc = vn[sl]
        acc = bs_ref[...]
        for h in range(GROUP_W // HEAD_DIM):
            vm = jnp.where(head == h, vc, 0.0).astype(BF16)
            acc = acc + jnp.dot(ws_ref[h], vm, preferred_element_type=F32)
        cmlp_ref[sl, :] = u_ref[sl, :] * acc


def _local_mixers(z, conv_w, cmlp_g, cmlp_ws, cmlp_bias):
    n = z.shape[0]
    tn = min(n, 256)
    tb = tn // SUBLANES
    nb8 = n // SUBLANES

    def col(j):
        return pl.BlockSpec((tn, GROUP_W), lambda i: (i, j))

    def prev(j):
        return pl.BlockSpec((SUBLANES, GROUP_W), lambda i: (jnp.maximum(i * tb - 1, 0), j))

    def nxt(j):
        return pl.BlockSpec((SUBLANES, GROUP_W), lambda i: (jnp.minimum((i + 1) * tb, nb8 - 1), j))

    out = pl.BlockSpec((tn, GROUP_W), lambda i: (i, 0))
    return pl.pallas_call(
        functools.partial(_local_kernel, tn=tn),
        grid=(n // tn,),
        in_specs=[col(0), col(1), col(2), col(3), col(4), prev(0), prev(2), nxt(0), nxt(2),
                  pl.BlockSpec((3, GROUP_W), lambda i: (0, 0)),
                  pl.BlockSpec((1, GROUP_W), lambda i: (0, 0)),
                  pl.BlockSpec((4, CHUNK, CHUNK), lambda i: (0, 0, 0)),
                  pl.BlockSpec((CHUNK, GROUP_W), lambda i: (0, 0))],
        out_specs=[out, out],
        out_shape=[jax.ShapeDtypeStruct((n, GROUP_W), F32)] * 2,
        compiler_params=_cparams(("arbitrary",)),
        name="local_mixers",
    )(z, z, z, z, z, z, z, z, z, conv_w, cmlp_g, cmlp_ws, cmlp_bias)


def _attprep_kernel(*refs, rope):
    if rope:
        q_ref, kv_ref, qg_ref, kg_ref, cos_ref, sin_ref, qo_ref, kt_ref, vd_ref = refs
    else:
        q_ref, kv_ref, qg_ref, kg_ref, qo_ref, kt_ref, vd_ref = refs
    q = q_ref[...]
    kv = kv_ref[...]
    k = kv[:, :LANES]
    v = kv[:, LANES:]
    same_head = (_iota((LANES, LANES), 0) >> 6) == (_iota((LANES, LANES), 1) >> 6)
    bd = jnp.where(same_head, 1.0 / HEAD_DIM, 0.0)
    lane = _iota((1, LANES), 1)
    first_half = (lane & 31) < 16

    def head_norm(x, g):
        ms = _dot_f32_lhs(x * x, bd)
        return x * lax.rsqrt(ms + EPS) * g

    def rotary(x):
        if not rope:
            return x
        rot = jnp.where(first_half, -pltpu.roll(x, LANES - 16, 1), pltpu.roll(x, 16, 1))
        return x * cos_ref[...] + rot * sin_ref[...]

    for half in range(2):
        sl = slice(half * LANES, (half + 1) * LANES)
        qh = rotary(head_norm(q[:, sl], qg_ref[...]))
        qo_ref[:, sl] = (qh * (HEAD_DIM ** -0.5 * LOG2E)).astype(BF16)
    kn = rotary(head_norm(k, kg_ref[...]))
    kt = kn.T.astype(BF16)
    kt_ref[0:64, :] = kt[0:64]
    kt_ref[64:128, :] = kt[0:64]
    kt_ref[128:192, :] = kt[64:128]
    kt_ref[192:256, :] = kt[64:128]
    vr = pltpu.roll(v, HEAD_DIM, 1)
    low = lane < HEAD_DIM
    vd_ref[:, 0:LANES] = jnp.where(low, v, vr).astype(BF16)
    vd_ref[:, LANES:] = jnp.where(low, vr, v).astype(BF16)


def _attprep(z, q_g, k_g, tables):
    n = z.shape[0]
    tn = min(n, 256)
    rope = tables is not None
    in_specs = [pl.BlockSpec((tn, GROUP_W), lambda i: (i, 5)),
                pl.BlockSpec((tn, GROUP_W), lambda i: (i, 6)),
                pl.BlockSpec((1, LANES), lambda i: (0, 0)),
                pl.BlockSpec((1, LANES), lambda i: (0, 0))]
    args = [z, z, q_g, k_g]
    if rope:
        in_specs += [pl.BlockSpec((tn, LANES), lambda i: (i, 0))] * 2
        args += list(tables)
    return pl.pallas_call(
        functools.partial(_attprep_kernel, rope=rope),
        grid=(n // tn,),
        in_specs=in_specs,
        out_specs=[pl.BlockSpec((tn, GROUP_W), lambda i: (i, 0)),
                   pl.BlockSpec((GROUP_W, tn), lambda i: (0, i)),
                   pl.BlockSpec((tn, GROUP_W), lambda i: (i, 0))],
        out_shape=[jax.ShapeDtypeStruct((n, GROUP_W), BF16),
                   jax.ShapeDtypeStruct((GROUP_W, n), BF16),
                   jax.ShapeDtypeStruct((n, GROUP_W), BF16)],
        compiler_params=_cparams(("arbitrary",)),
        name="attprep_rope" if rope else "attprep",
    )(*args)


def _flash_kernel(q_ref, k_ref, v_ref, o_ref, s_ref, mx_ref, *, tk, nk):
    q = q_ref[...]
    tq = q.shape[0]
    low = _iota((1, LANES), 1) < HEAD_DIM
    zero = jnp.zeros_like(q)
    qa = jnp.where(low, q, zero)
    qb = jnp.where(low, zero, q)

    def scores(j, slot):
        k = k_ref[:, pl.ds(pl.multiple_of(j * tk, LANES), tk)]
        for h, qh in enumerate((qa, qb)):
            s = jnp.dot(qh, k, preferred_element_type=F32)
            s_ref[slot, h] = s
            mx_ref[slot, h] = jnp.max(s, axis=1, keepdims=True)

    def consume(j, slot, carry):
        m_a, acc_a, m_b, acc_b = carry
        v = v_ref[pl.ds(pl.multiple_of(j * tk, LANES), tk), :]
        ones = jnp.ones_like(v)
        n_a = jnp.maximum(m_a, mx_ref[slot, 0])
        n_b = jnp.maximum(m_b, mx_ref[slot, 1])
        p_a = jnp.exp2(s_ref[slot, 0] - n_a).astype(BF16)
        p_b = jnp.exp2(s_ref[slot, 1] - n_b).astype(BF16)
        acc_a = acc_a * jnp.exp2(m_a - n_a) + jnp.dot(p_a, jnp.where(low, v, ones), preferred_element_type=F32)
        acc_b = acc_b * jnp.exp2(m_b - n_b) + jnp.dot(p_b, jnp.where(low, ones, v), preferred_element_type=F32)
        return n_a, acc_a, n_b, acc_b

    neg = jnp.full((tq, 1), -jnp.inf, F32)
    za = jnp.zeros((tq, LANES), F32)
    carry = (neg, za, neg, za)
    scores(0, 0)
    if nk % 2 == 0 and nk >= 4:
        def pair(i, carry):
            j = 2 * i
            scores(j + 1, 1)
            carry = consume(j, 0, carry)
            scores(j + 2, 0)
            return consume(j + 1, 1, carry)

        carry = lax.fori_loop(0, nk // 2 - 1, pair, carry)
        scores(nk - 1, 1)
        carry = consume(nk - 2, 0, carry)
        carry = consume(nk - 1, 1, carry)
    else:
        for j in range(nk):
            if j + 1 < nk:
                scores(j + 1, (j + 1) % 2)
            carry = consume(j, j % 2, carry)
    _, acc_a, _, acc_b = carry
    o_ref[...] = jnp.where(low, acc_a / pltpu.roll(acc_a, HEAD_DIM, 1), acc_b / pltpu.roll(acc_b, HEAD_DIM, 1))


def _kv_tile(nk):
    best = LANES
    for t in range(LANES, FLASH_TK_MAX + 1, LANES):
        if nk % t == 0:
            best = t
    return best


def _flash(q, kt, vd):
    n = q.shape[0]
    nk = kt.shape[1]
    tq = min(n, FLASH_TQ)
    tk = _kv_tile(nk)
    return pl.pallas_call(
        functools.partial(_flash_kernel, tk=tk, nk=nk // tk),
        grid=(2, n // tq),
        in_specs=[pl.BlockSpec((tq, LANES), lambda g, i: (i, g)),
                  pl.BlockSpec((LANES, nk), lambda g, i: (g, 0)),
                  pl.BlockSpec((nk, LANES), lambda g, i: (0, g))],
        out_specs=pl.BlockSpec((tq, LANES), lambda g, i: (i, g)),
        out_shape=jax.ShapeDtypeStruct((n, GROUP_W), F32),
        scratch_shapes=[pltpu.VMEM((2, 2, tq, tk), F32), pltpu.VMEM((2, 2, tq, 1), F32)],
        compiler_params=_cparams(("arbitrary", "arbitrary")),
        name="flash_attention",
    )(q, kt, vd)


def _mlstm_chunk(q, k, v, g, h_ref, c_ref, n_ref, m_ref, *, fwd):
    off = 0 if fwd else 4
    nh = GROUP_W // HEAD_DIM
    ks = k * (HEAD_DIM ** -0.5)
    lane = _iota((1, LANES), 1)
    lsig = jnp.minimum(g, 0.0) - jnp.log(1.0 + jnp.exp(-jnp.abs(g)))
    gcol = jnp.where((lane >= 8) & (lane < 16), lsig, g)
    lfcol = pltpu.roll(gcol, LANES - 8, 1)
    rr = _iota((CHUNK, CHUNK), 0)
    cc = _iota((CHUNK, CHUNK), 1)
    sees = (cc <= rr) if fwd else (cc >= rr)
    tri = jnp.where(sees, 1.0, 0.0)
    bc = _dot_f32_rhs(tri, lfcol)
    bct = bc.T
    gt = gcol.T
    total = bc[CHUNK - 1:CHUNK, :] if fwd else bc[0:1, :]
    cp = c_ref[...]
    npv = n_ref[0:1, :]
    mp = m_ref[0:1, :]
    head = _iota((1, GROUP_W), 1) >> 6
    qb = q.astype(BF16)
    kb = ks.astype(BF16)
    qc = lax.dot_general(qb, cp.astype(BF16), NT_DIMS, preferred_element_type=F32)
    heads = range(nh)
    stack = lambda parts: jnp.concatenate(parts, axis=0)
    restack = lambda x: stack([x[:, h * CHUNK:(h + 1) * CHUNK] for h in heads])
    pick = jnp.where(_iota((LANES, nh * CHUNK), 0) == off + (_iota((LANES, nh * CHUNK), 1) >> 7), 1.0, 0.0)
    eh = jnp.where((_iota((GROUP_W, nh * CHUNK), 0) >> 6) == (_iota((GROUP_W, nh * CHUNK), 1) >> 7), 1.0, 0.0)
    nq = restack(jnp.dot((q * npv).astype(BF16), eh.astype(BF16), preferred_element_type=F32))
    spread = _dot_f32_lhs(stack([bc, bc + mp]), pick)
    bc_t = restack(spread[0:CHUNK])
    inter = restack(spread[CHUNK:2 * CHUNK])
    row = stack([jnp.broadcast_to(gt[off + h:off + h + 1, :] - bct[off + h:off + h + 1, :],
                                  (CHUNK, CHUNK)) for h in heads])
    dm = jnp.where(stack([sees] * nh), bc_t + row, -jnp.inf)
    m_t = jnp.maximum(inter, jnp.max(dm, axis=1, keepdims=True))
    q_stack = stack([jnp.where(head == h, q, 0.0) for h in heads]).astype(BF16)
    v_stack = stack([jnp.where(head == h, v, 0.0) for h in heads]).astype(BF16)
    s = lax.dot_general(q_stack, kb, NT_DIMS, preferred_element_type=F32) * jnp.exp(dm - m_t)
    wi = jnp.exp(inter - m_t)
    s_lanes = jnp.concatenate([s[h * CHUNK:(h + 1) * CHUNK] for h in heads], axis=1).astype(BF16)
    num = jnp.dot(s_lanes, v_stack, preferred_element_type=F32)
    den = jnp.sum(s, axis=1, keepdims=True) + wi * nq
    dd = jnp.maximum(jnp.abs(den), jnp.exp(-m_t))
    wi_x = jnp.zeros((CHUNK, GROUP_W), F32)
    den_x = jnp.ones((CHUNK, GROUP_W), F32)
    for h in heads:
        rows = slice(h * CHUNK, (h + 1) * CHUNK)
        wi_x = jnp.where(head == h, jnp.concatenate([wi[rows]] * 2, axis=1), wi_x)
        den_x = jnp.where(head == h, jnp.concatenate([dd[rows]] * 2, axis=1), den_x)
    h_ref[...] = (num + wi_x * qc) / den_x

    a = total - bc + gcol
    m_loc = jnp.max(a, axis=0, keepdims=True)
    w = jnp.exp(a - m_loc)
    m_new = jnp.maximum(total + mp, m_loc)
    d_old = jnp.exp(total + mp - m_new)
    d_loc = jnp.exp(m_loc - m_new)
    pick_w = jnp.where(_iota((LANES, GROUP_W), 0) == off + (_iota((LANES, GROUP_W), 1) >> 6), 1.0, 0.0)
    w_x = _dot_f32_lhs(w, pick_w)
    decay = _dot_f32_lhs(stack([d_old, d_loc, jnp.zeros((SUBLANES - 2, LANES), F32)]), pick_w)
    dold_x = decay[0:1]
    dloc_x = decay[1:2]
    c_loc = jnp.dot((w_x * v).T.astype(BF16), kb, preferred_element_type=F32)
    same_head = (_iota((GROUP_W, GROUP_W), 0) >> 6) == (_iota((GROUP_W, GROUP_W), 1) >> 6)
    c_ref[...] = cp * dold_x + jnp.where(same_head, c_loc, 0.0) * dloc_x
    n_new = npv * dold_x + jnp.sum(w_x * ks, axis=0, keepdims=True) * dloc_x
    n_ref[...] = jnp.broadcast_to(n_new, n_ref.shape)
    m_ref[...] = jnp.broadcast_to(m_new, m_ref.shape)


def _mlstm_kernel(qf_ref, kf_ref, vf_ref, gf_ref, qb_ref, kb_ref, vb_ref, gb_ref, bias_ref,
                  c0f_ref, n0f_ref, m0f_ref, c0b_ref, n0b_ref, m0b_ref,
                  hf_ref, hb_ref, cf_ref, nf_ref, mf_ref, cb_ref, nb_ref, mb_ref):
    @pl.when(pl.program_id(0) == 0)
    def _():
        for dst, src in ((cf_ref, c0f_ref), (nf_ref, n0f_ref), (mf_ref, m0f_ref),
                         (cb_ref, c0b_ref), (nb_ref, n0b_ref), (mb_ref, m0b_ref)):
            dst[...] = src[...]

    _mlstm_chunk(qf_ref[...], kf_ref[...], vf_ref[...], gf_ref[...] + bias_ref[...],
                 hf_ref, cf_ref, nf_ref, mf_ref, fwd=True)
    _mlstm_chunk(qb_ref[...], kb_ref[...], vb_ref[...], gb_ref[...] + bias_ref[...],
                 hb_ref, cb_ref, nb_ref, mb_ref, fwd=False)


def _mlstm(z, gate_b, state_f, state_b):
    n = z.shape[0]
    nc = n // CHUNK

    def cols(order):
        return [pl.BlockSpec((CHUNK, GROUP_W), lambda c, j=j: (order(c), j)) for j in (7, 8, 9)] + [
            pl.BlockSpec((CHUNK, LANES), lambda c: (order(c), N_IN_BLOCKS - 1))]

    def full(shape):
        return pl.BlockSpec(shape, lambda c: (0,) * len(shape))

    fo = lambda c: c
    bo = lambda c: nc - 1 - c
    st_shapes = [(GROUP_W, GROUP_W), (SUBLANES, GROUP_W), (SUBLANES, LANES)]
    st_specs = [full(s) for s in st_shapes]
    outs = pl.pallas_call(
        _mlstm_kernel,
        grid=(nc,),
        in_specs=cols(fo) + cols(bo) + [full((1, LANES))] + st_specs * 2,
        out_specs=[pl.BlockSpec((CHUNK, GROUP_W), lambda c: (fo(c), 0)),
                   pl.BlockSpec((CHUNK, GROUP_W), lambda c: (bo(c), 0))] + st_specs * 2,
        out_shape=[jax.ShapeDtypeStruct((n, GROUP_W), F32)] * 2
        + [jax.ShapeDtypeStruct(s, F32) for s in st_shapes] * 2,
        compiler_params=_cparams(("arbitrary",)),
        name="mlstm",
    )(z, z, z, z, z, z, z, z, gate_b, *state_f, *state_b)
    return outs[0], outs[1], tuple(outs[2:5]), tuple(outs[5:8])


def _outproj_kernel(cv_ref, cm_ref, at_ref, hf_ref, hb_ref, po_ref, mg_ref, w_ref, x_ref, g1_ref, ng_ref,
                    sh_ref, sc_ref, rw_ref, xo_ref, h_ref, aff_ref, *, n_exp):
    hs = hf_ref[...] + hb_ref[...]
    same_head = (_iota((GROUP_W, GROUP_W), 0) >> 6) == (_iota((GROUP_W, GROUP_W), 1) >> 6)
    hms = _dot_f32_lhs(hs * hs, jnp.where(same_head, 1.0 / HEAD_DIM, 0.0))
    ml = _sigmoid(po_ref[...]) * (hs * lax.rsqrt(hms + EPS) * mg_ref[...])
    mix = jnp.zeros(x_ref.shape, F32)
    for j, val in enumerate((cv_ref[...], cm_ref[...], at_ref[...], ml)):
        mix = mix + jnp.dot(val.astype(BF16), w_ref[j * GROUP_W:(j + 1) * GROUP_W, :],
                            preferred_element_type=F32)
    x = x_ref[...] + g1_ref[...] * mix
    xo_ref[...] = x
    ms = jnp.mean(x * x, axis=-1, keepdims=True)
    h = x * lax.rsqrt(ms + EPS) * ng_ref[...]
    h = h * (1.0 + sc_ref[...]) + sh_ref[...]
    h_ref[...] = h.astype(BF16)
    h_hi, h_lo = _split(h, 2)
    w_hi, w_lo = _split(rw_ref[...], 2)
    logits = (jnp.dot(h_hi, w_hi, preferred_element_type=F32) + jnp.dot(h_lo, w_hi, preferred_element_type=F32)
              + jnp.dot(h_hi, w_lo, preferred_element_type=F32))
    lt = logits.T[0:n_exp, :]
    e = jnp.exp(lt - jnp.max(lt, axis=0, keepdims=True))
    aff_ref[...] = e / jnp.sum(e, axis=0, keepdims=True)


def _outproj(mixers, z, ml_norm_g, w_out, x, gate1, norm_g, shift, scale, router_pad, n_exp):
    n, d = x.shape
    tn = min(n, 512)
    mix_spec = pl.BlockSpec((tn, GROUP_W), lambda i: (i, 0))
    row = pl.BlockSpec((tn, d), lambda i: (i, 0))
    vec = pl.BlockSpec((1, d), lambda i: (0, 0))
    return pl.pallas_call(
        functools.partial(_outproj_kernel, n_exp=n_exp),
        grid=(n // tn,),
        in_specs=[mix_spec] * 5 + [pl.BlockSpec((tn, GROUP_W), lambda i: (i, 10)),
                                   pl.BlockSpec((1, GROUP_W), lambda i: (0, 0)),
                                   pl.BlockSpec((d, d), lambda i: (0, 0)), row, vec, vec, vec, vec,
                                   pl.BlockSpec((d, LANES), lambda i: (0, 0))],
        out_specs=[row, row, pl.BlockSpec((n_exp, tn), lambda i: (0, i))],
        out_shape=[jax.ShapeDtypeStruct((n, d), F32), jax.ShapeDtypeStruct((n, d), BF16),
                   jax.ShapeDtypeStruct((n_exp, n), F32)],
        compiler_params=_cparams(("arbitrary",)),
        name="outproj_router",
    )(*mixers, z, ml_norm_g, w_out, x, gate1, norm_g, shift, scale, router_pad)


def _route_kernel(aff_ref, rank_ref, offs_ref, *, cap, iters):
    aff = aff_ref[...]
    n_exp = aff.shape[0]

    def count(mask):
        c = jnp.where(mask, 1.0, 0.0)
        return jnp.sum(jnp.sum(c, axis=1, keepdims=True), axis=2, keepdims=True)

    def bisect(_, carry):
        lo, hi = carry
        mid = 0.5 * (lo + hi)
        ok = count(aff >= mid) >= cap
        return jnp.where(ok, mid, lo), jnp.where(ok, hi, mid)

    _, hi = lax.fori_loop(0, iters, bisect,
                          (jnp.zeros((n_exp, 1, 1), F32), jnp.full((n_exp, 1, 1), 2.0, F32)))
    below = jnp.where(aff < hi, aff, -1.0)
    thr = jnp.max(jnp.max(below, axis=1, keepdims=True), axis=2, keepdims=True)
    gt = aff > thr
    eq = aff == thr
    need = cap - count(gt)

    rr = _iota((LANES, LANES), 0)
    cc = _iota((LANES, LANES), 1)
    upper = jnp.where(rr <= cc, 1.0, 0.0).astype(BF16)
    lower = jnp.where(cc < rr, 1.0, 0.0).astype(BF16)

    def prefix(x01):
        within = jnp.dot(x01.astype(BF16), upper, preferred_element_type=F32)
        offs = jnp.dot(lower, within.astype(BF16), preferred_element_type=F32)[:, LANES - 1:LANES]
        return within - x01 + offs, offs

    offs_all = jnp.zeros((ROUTE_BLOCKS, LANES), F32)
    for e in range(n_exp):
        gt_e = jnp.where(gt[e], 1.0, 0.0)
        eq_e = jnp.where(eq[e], 1.0, 0.0)
        eq_rank, _ = prefix(eq_e)
        sel = jnp.maximum(gt_e, jnp.where(eq_rank < need[e], eq_e, 0.0))
        rk, offs = prefix(sel)
        rk = jnp.where(sel > 0.0, rk, -1.0)
        rank_ref[e] = rk.astype(jnp.int32)
        taken = jnp.sum(jnp.sum(sel, axis=0, keepdims=True), axis=1, keepdims=True)
        offs_all = jnp.where(cc == e, offs, offs_all)
        offs_all = jnp.where(cc == n_exp + e, taken, offs_all)
    offs_ref[...] = offs_all.astype(jnp.int32)


def _route(aff_t, cap):
    n_exp, n = aff_t.shape
    assert n % LANES == 0 and n <= ROUTE_BLOCKS * LANES and cap <= n
    a = aff_t.reshape(n_exp, n // LANES, LANES)
    if n < ROUTE_BLOCKS * LANES:
        a = jnp.pad(a, ((0, 0), (0, ROUTE_BLOCKS - n // LANES), (0, 0)), constant_values=-1.0)
    blk = pl.BlockSpec((n_exp, ROUTE_BLOCKS, LANES), lambda i: (0, 0, 0))
    rank, offs = pl.pallas_call(
        functools.partial(_route_kernel, cap=cap, iters=ROUTE_ITERS),
        grid=(1,),
        in_specs=[blk],
        out_specs=[blk, pl.BlockSpec((ROUTE_BLOCKS, LANES), lambda i: (0, 0))],
        out_shape=[jax.ShapeDtypeStruct((n_exp, ROUTE_BLOCKS, LANES), jnp.int32),
                   jax.ShapeDtypeStruct((ROUTE_BLOCKS, LANES), jnp.int32)],
        compiler_params=_cparams(("arbitrary",)),
        name="route",
    )(a)
    offs_flat = jnp.concatenate([offs[:, :n_exp].T.reshape(-1), offs[0, n_exp:2 * n_exp]])
    return a, rank, offs_flat


def _gather_kernel(offs_ref, h_ref, rank_ref, gate_ref, xs_ref, gs_ref, idx_ref, acc_ref, gacc_ref, iacc_ref,
                   *, cap, spb):
    e = pl.program_id(0)
    sb = pl.program_id(1)

    @pl.when(sb == 0)
    def _():
        acc_ref[...] = jnp.zeros(acc_ref.shape, F32)
        gacc_ref[...] = jnp.zeros(gacc_ref.shape, F32)
        iacc_ref[...] = jnp.zeros(iacc_ref.shape, F32)

    lane = _iota((1, LANES), 1)

    def place(k, b, r0a, win):
        rows = _iota((win, LANES), 0)
        rrow = rank_ref[0, pl.ds(b, 1), :]
        grow = gate_ref[0, pl.ds(b, 1), :]
        trow = (b * LANES + lane).astype(F32)
        hit = (rrow - r0a) == rows
        hblk = h_ref[pl.ds(pl.multiple_of(k * LANES, LANES), LANES), :]
        onehot = jnp.where(hit, 1.0, 0.0).astype(BF16)
        rows_x = jnp.dot(onehot, hblk, preferred_element_type=F32)
        rows_g = jnp.sum(jnp.where(hit, grow, 0.0), axis=1, keepdims=True)
        rows_i = jnp.sum(jnp.where(hit, trow, 0.0), axis=1, keepdims=True)
        head = pl.ds(r0a, SUBLANES)
        tail = pl.ds(r0a + SUBLANES, win - SUBLANES)
        for ref, val in ((acc_ref, rows_x), (gacc_ref, rows_g), (iacc_ref, rows_i)):
            ref[head, :] += val[0:SUBLANES]
            ref[tail, :] = val[SUBLANES:win]

    group = min(spb, GATHER_GROUP)

    def body(i, carry):
        ks = [i * group + u for u in range(group)]
        bs = [sb * spb + k for k in ks]
        r0as, short = [], None
        for b in bs:
            r0 = offs_ref[e * ROUTE_BLOCKS + b]
            r0a = pl.multiple_of((r0 >> 3) << 3, SUBLANES)
            nxt = offs_ref[jnp.where(b == ROUTE_BLOCKS - 1, pl.num_programs(0) * ROUTE_BLOCKS + e,
                                     e * ROUTE_BLOCKS + b + 1)]
            fits = nxt - r0a <= GATHER_WIN_SHORT
            short = fits if short is None else short & fits
            r0as.append(r0a)

        def run(win):
            for k, b, r0a in zip(ks, bs, r0as):
                place(k, b, r0a, win)

        lax.cond(short, lambda: run(GATHER_WIN_SHORT), lambda: run(GATHER_WIN))
        return carry

    lax.fori_loop(0, spb // group, body, 0)

    @pl.when(sb == pl.num_programs(1) - 1)
    def _():
        xs_ref[0] = acc_ref[0:cap, :].astype(BF16)
        gs_ref[0] = gacc_ref[0:cap, :]
        idx_ref[0] = iacc_ref[0:cap, :].astype(jnp.int32)


def _gather(offs_flat, h2, rank, gate, cap):
    n, d = h2.shape
    n_exp = rank.shape[0]
    sbt = min(n, 2048)
    spb = sbt // LANES
    blk = pl.BlockSpec((1, ROUTE_BLOCKS, LANES), lambda e, s, offs: (e, 0, 0))
    return pl.pallas_call(
        functools.partial(_gather_kernel, cap=cap, spb=spb),
        grid_spec=pltpu.PrefetchScalarGridSpec(
            num_scalar_prefetch=1,
            grid=(n_exp, n // sbt),
            in_specs=[pl.BlockSpec((sbt, d), lambda e, s, offs: (s, 0)), blk, blk],
            out_specs=[pl.BlockSpec((1, cap, d), lambda e, s, offs: (e, 0, 0)),
                       pl.BlockSpec((1, cap, 1), lambda e, s, offs: (e, 0, 0)),
                       pl.BlockSpec((1, cap, 1), lambda e, s, offs: (e, 0, 0))],
            scratch_shapes=[pltpu.VMEM((cap + GATHER_WIN, d), F32),
                            pltpu.VMEM((cap + GATHER_WIN, 1), F32),
                            pltpu.VMEM((cap + GATHER_WIN, 1), F32)]),
        out_shape=[jax.ShapeDtypeStruct((n_exp, cap, d), BF16),
                   jax.ShapeDtypeStruct((n_exp, cap, 1), F32),
                   jax.ShapeDtypeStruct((n_exp, cap, 1), jnp.int32)],
        compiler_params=_cparams(("arbitrary", "arbitrary")),
        name="moe_gather",
    )(offs_flat, h2, rank, gate)


def _ffn_tiles(caps, tm):
    tiles = [[(0, a, min(tm, caps[0] - a))] for a in range(0, caps[0], tm)]
    for s in range(1, len(caps)):
        if caps[s] <= FFN_RIDER_ROWS:
            tiles[-1].append((s, 0, caps[s]))
        else:
            tiles += [[(s, a, min(tm, caps[s] - a))] for a in range(0, caps[s], tm)]
    return tiles


def _ffn_kernel(*refs, caps, tm):
    ns = len(caps)
    xs_refs, (w1_ref, w3_ref, w2_ref) = refs[:ns], refs[ns:ns + 3]
    gs_refs, o_refs = refs[ns + 3:2 * ns + 3], refs[2 * ns + 3:]
    f = pl.program_id(1)

    @pl.when(f == 0)
    def _():
        for o_ref in o_refs:
            o_ref[...] = jnp.zeros(o_ref.shape, F32)

    w1 = w1_ref[0, 0].astype(BF16)
    w3 = w3_ref[0, 0].astype(BF16)
    w2 = w2_ref[0, 0].astype(BF16)
    for tile in _ffn_tiles(caps, tm):
        parts = [xs_refs[s][0, a:a + n, :] for s, a, n in tile]
        x = parts[0] if len(parts) == 1 else jnp.concatenate(parts, axis=0)
        a_ = jnp.dot(x, w1, preferred_element_type=F32)
        b_ = jnp.dot(x, w3, preferred_element_type=F32)
        hid = (a_ * _sigmoid(a_) * b_).astype(BF16)
        y = jnp.dot(hid, w2, preferred_element_type=F32)
        r = 0
        for s, a, n in tile:
            o_refs[s][0, a:a + n, :] += y[r:r + n]
            r += n

    @pl.when(f == pl.num_programs(1) - 1)
    def _():
        for o_ref, gs_ref in zip(o_refs, gs_refs):
            o_ref[0] = o_ref[0] * gs_ref[0]


def _ffn(row_sets, w1, w3, w2, layer):
    caps = tuple(xs.shape[1] for xs, _ in row_sets)
    n_exp, _, d = row_sets[0][0].shape
    ff = w1.shape[3]
    tf = 512
    tm = min(caps[0], 512)
    rows = lambda cap, w: pl.BlockSpec((1, cap, w), lambda e, f: (e, 0, 0))
    outs = pl.pallas_call(
        functools.partial(_ffn_kernel, caps=caps, tm=tm),
        grid=(n_exp, ff // tf),
        in_specs=[rows(c, d) for c in caps]
        + [pl.BlockSpec((1, 1, d, tf), lambda e, f: (layer, e, 0, f)),
           pl.BlockSpec((1, 1, d, tf), lambda e, f: (layer, e, 0, f)),
           pl.BlockSpec((1, 1, tf, d), lambda e, f: (layer, e, f, 0))]
        + [rows(c, 1) for c in caps],
        out_specs=[rows(c, d) for c in caps],
        out_shape=[jax.ShapeDtypeStruct((n_exp, c, d), F32) for c in caps],
        compiler_params=_cparams(("arbitrary", "arbitrary")),
        name="moe_ffn",
    )(*[xs for xs, _ in row_sets], w1, w3, w2, *[gs for _, gs in row_sets])
    return outs


def _combine_kernel(offs_ref, idx_ref, out_ref, y_ref, *, cap, tht):
    th = pl.program_id(0)
    e = pl.program_id(1)

    @pl.when(e == 0)
    def _():
        y_ref[...] = jnp.zeros(y_ref.shape, F32)

    bpt = tht // LANES
    first = e * ROUTE_BLOCKS + th * bpt
    start = offs_ref[first]
    is_last = th == pl.num_programs(0) - 1
    taken = offs_ref[pl.num_programs(1) * ROUTE_BLOCKS + e]
    end = jnp.where(is_last, taken, offs_ref[jnp.minimum(first + bpt, (e + 1) * ROUTE_BLOCKS - 1)])
    base = th * tht

    def token(r):
        return jnp.clip(idx_ref[e * cap + r] - base, 0, tht - 1)

    def group(i, carry):
        r = start + COMBINE_GROUP * i
        ts = [token(r + u) for u in range(COMBINE_GROUP)]
        sums = [y_ref[pl.ds(ts[u], 1), :] + out_ref[0, pl.ds(r + u, 1), :] for u in range(COMBINE_GROUP)]
        for u in range(COMBINE_GROUP):
            y_ref[pl.ds(ts[u], 1), :] = sums[u]
        return carry

    def single(r, carry):
        y_ref[pl.ds(token(r), 1), :] += out_ref[0, pl.ds(r, 1), :]
        return carry

    groups = lax.div(end - start, jnp.int32(COMBINE_GROUP))
    lax.fori_loop(0, groups, group, 0)
    lax.fori_loop(start + groups * COMBINE_GROUP, end, single, 0)


def _combine(offs_flat, idx_flat, out, n):
    n_exp, cap, d = out.shape
    tht = min(n, 4096)
    return pl.pallas_call(
        functools.partial(_combine_kernel, cap=cap, tht=tht),
        grid_spec=pltpu.PrefetchScalarGridSpec(
            num_scalar_prefetch=2,
            grid=(n // tht, n_exp),
            in_specs=[pl.BlockSpec((1, cap, d), lambda t, e, offs, idx: (e, 0, 0))],
            out_specs=pl.BlockSpec((tht, d), lambda t, e, offs, idx: (t, 0))),
        out_shape=jax.ShapeDtypeStruct((n, d), F32),
        compiler_params=_cparams(("arbitrary", "arbitrary")),
        name="moe_combine",
    )(offs_flat, idx_flat, out)


def _moe(routed_sets, w1, w3, w2, layer):
    plans = []
    for h2, aff_t in routed_sets:
        n = h2.shape[0]
        cap = CAPACITY_FACTOR * n // aff_t.shape[0]
        gate, rank, offs_flat = _route(aff_t, cap)
        xs, gs, idx = _gather(offs_flat, h2, rank, gate, cap)
        plans.append((offs_flat, idx.reshape(-1), xs, gs, n))
    outs = _ffn([(p[2], p[3]) for p in plans], w1, w3, w2, layer)
    return [_combine(p[0], p[1], out, p[4]) for p, out in zip(plans, outs)]


def _final_kernel(x_ref, y_ref, g2_ref, ng_ref, o_ref):
    x = x_ref[...] + g2_ref[...] * y_ref[...]
    ms = jnp.mean(x * x, axis=-1, keepdims=True)
    o_ref[...] = x * lax.rsqrt(ms + EPS) * ng_ref[...]


def _final(x, y, g2, norm_g):
    n, d = x.shape
    tn = min(n, 512)
    row = pl.BlockSpec((tn, d), lambda i: (i, 0))
    vec = pl.BlockSpec((1, d), lambda i: (0, 0))
    return pl.pallas_call(
        _final_kernel,
        grid=(n // tn,),
        in_specs=[row, row, vec, vec],
        out_specs=row,
        out_shape=jax.ShapeDtypeStruct((n, d), F32),
        compiler_params=_cparams(("arbitrary",)),
        name="final_norm",
    )(x, y, g2, norm_g)


def _rope_tables(n):
    pos = jnp.arange(n)
    row, colp = pos // GRID_W, pos % GRID_W
    axis_dim = HEAD_DIM // 2
    inv_freq = ROPE_THETA ** (-jnp.arange(0, axis_dim, 2, dtype=F32) / axis_dim)

    def axis_angles(p):
        a = p.astype(F32)[:, None] * inv_freq[None, :]
        return jnp.concatenate([a, a], axis=-1)

    ang = jnp.concatenate([axis_angles(row), axis_angles(colp)], axis=-1)
    ang = jnp.concatenate([ang, ang], axis=-1)
    return jnp.cos(ang), jnp.sin(ang)


def _zero_state():
    return (jnp.zeros((GROUP_W, GROUP_W), F32), jnp.zeros((SUBLANES, GROUP_W), F32),
            jnp.zeros((SUBLANES, LANES), F32))


def kernel(x, c, ctx, c_ctx, ada_w, ada_b, norm1_g, w_in, conv_w, cmlp_norm_g, cmlp_ws, cmlp_bs, q_norm_g, k_norm_g, ml_igate_b, ml_fgate_b, ml_norm_g, w_out, norm2_g, router_w, exp_w1, exp_w3, exp_w2, final_norm_g):
    assert x.shape[0] == 1 and ctx.shape[0] == 1
    depth, d, d_in = w_in.shape
    n_exp = router_w.shape[-1]
    assert d_in <= N_IN_BLOCKS * LANES and d == 4 * GROUP_W
    xs = x[0]
    xc = ctx[0]
    n = xs.shape[0]

    mod = _modulation(jnp.stack([c[0], c_ctx], axis=1), ada_w, ada_b)
    tables = _rope_tables(n)
    row = lambda v: v.reshape(1, -1)

    res_x = None
    res_c = None
    for l in range(depth):
        parts = [[mod[l, s:s + 1, j * d:(j + 1) * d] for j in range(6)] for s in range(2)]
        sh1, sc1, g1, sh2, sc2, g2 = parts[0]
        csh1, csc1, cg1, csh2, csc2, cg2 = parts[1]
        w_in_l = jnp.pad(w_in[l], ((0, 0), (0, N_IN_BLOCKS * LANES - d_in))).astype(BF16)
        w_out_l = w_out[l].astype(BF16)
        cmlp_ws_l = cmlp_ws[l].astype(BF16)
        cmlp_bias = jnp.repeat(cmlp_bs[l].T, HEAD_DIM, axis=1)
        qg = jnp.tile(q_norm_g[l], 2).reshape(1, LANES)
        kg = jnp.tile(k_norm_g[l], 2).reshape(1, LANES)
        gate_b = jnp.pad(jnp.concatenate([ml_igate_b[l], ml_fgate_b[l]]), (0, LANES - 16)).reshape(1, LANES)
        router_pad = jnp.pad(router_w[l], ((0, 0), (0, LANES - n_exp)))
        update_ctx = l < depth - 1

        zc, xc = _inproj(xc, res_c, row(norm1_g[l]), csh1, csc1, w_in_l)
        qc_, ktc, vdc = _attprep(zc, qg, kg, None)
        hcf, hcb, st_f, st_b = _mlstm(zc, gate_b, _zero_state(), _zero_state())
        if update_ctx:
            conv_c, cmlp_c = _local_mixers(zc, conv_w[l], row(cmlp_norm_g[l]), cmlp_ws_l, cmlp_bias)
            att_c = _flash(qc_, ktc, vdc)
            xc, hc2, affc = _outproj((conv_c, cmlp_c, att_c, hcf, hcb), zc, row(ml_norm_g[l]), w_out_l, xc, cg1,
                                     row(norm2_g[l]), csh2, csc2, router_pad, n_exp)

        z, xs = _inproj(xs, res_x, row(norm1_g[l]), sh1, sc1, w_in_l)
        conv_x, cmlp_x = _local_mixers(z, conv_w[l], row(cmlp_norm_g[l]), cmlp_ws_l, cmlp_bias)
        q_, kt, vd = _attprep(z, qg, kg, tables)
        att_x = _flash(q_, jnp.concatenate([ktc, kt], axis=1), jnp.concatenate([vdc, vd], axis=0))
        hf, hb, _, _ = _mlstm(z, gate_b, st_f, st_b)
        xs, h2, aff = _outproj((conv_x, cmlp_x, att_x, hf, hb), z, row(ml_norm_g[l]), w_out_l, xs, g1,
                               row(norm2_g[l]), sh2, sc2, router_pad, n_exp)
        routed = [(h2, aff)] + ([(hc2, affc)] if update_ctx else [])
        ys = _moe(routed, exp_w1, exp_w3, exp_w2, l)
        res_x = (ys[0], g2)
        if update_ctx:
            res_c = (ys[1], cg2)

    return _final(xs, res_x[0], res_x[1], row(final_norm_g))[None]
```

```python
import functools

import jax
import jax.numpy as jnp
from jax import lax
from jax.experimental import pallas as pl
from jax.experimental.pallas import tpu as pltpu

F32 = jnp.float32
BF16 = jnp.bfloat16
HIGHEST = lax.Precision.HIGHEST

EPS = 1e-6
LOG2E = 1.4426950408889634
GRID_W = 64
ROPE_THETA = 10000.0
CAPACITY_FACTOR = 2
GROUP_W = 256
HEAD_DIM = 64
CHUNK = 128
N_IN_BLOCKS = 23

LANES = 128
SUBLANES = 8
VMEM_LIMIT = 56 * 1024 * 1024

ROUTE_BLOCKS = 128
ROUTE_ITERS = 48
GATHER_ALIGN = 16
GATHER_WIN = 160
GATHER_WIN_SHORT = 64
GATHER_GROUP = 4
COMBINE_GROUP = 4
FFN_RIDER_ROWS = 64
FLASH_TQ = 512
FLASH_TK_MAX = 3328

NT_DIMS = (((1,), (1,)), ((), ()))


def _cparams(sem, vmem=None):
    return pltpu.CompilerParams(dimension_semantics=sem, vmem_limit_bytes=vmem or VMEM_LIMIT)


def _sigmoid(x):
    return 1.0 / (1.0 + jnp.exp(-x))


def _iota(shape, dim):
    return lax.broadcasted_iota(jnp.int32, shape, dim)


def _split(x, pieces):
    out = []
    for _ in range(pieces - 1):
        p = x.astype(BF16)
        out.append(p)
        x = x - p.astype(F32)
    return out + [x.astype(BF16)]


def _dot_f32_lhs(x, w):
    w = w.astype(BF16)
    return sum(jnp.dot(p, w, preferred_element_type=F32) for p in _split(x, 3))


def _dot_f32_rhs(w, x):
    w = w.astype(BF16)
    return sum(jnp.dot(w, p, preferred_element_type=F32) for p in _split(x, 3))


def _mod_kernel(cc_ref, w_ref, b_ref, o_ref):
    cc = cc_ref[...]
    s = cc * _sigmoid(cc)
    w = w_ref[0]
    b = b_ref[0]
    o_ref[0, 0:1, :] = jnp.sum(w * s[:, 0:1], axis=0, keepdims=True) + b
    o_ref[0, 1:2, :] = jnp.sum(w * s[:, 1:2], axis=0, keepdims=True) + b


def _modulation(cc, ada_w, ada_b):
    depth, d, m = ada_w.shape
    tn = 1536
    return pl.pallas_call(
        _mod_kernel,
        grid=(depth, m // tn),
        in_specs=[pl.BlockSpec((d, 2), lambda l, j: (0, 0)),
                  pl.BlockSpec((1, d, tn), lambda l, j: (l, 0, j)),
                  pl.BlockSpec((1, 1, tn), lambda l, j: (l, 0, j))],
        out_specs=pl.BlockSpec((1, 2, tn), lambda l, j: (l, 0, j)),
        out_shape=jax.ShapeDtypeStruct((depth, 2, m), F32),
        compiler_params=_cparams(("arbitrary", "arbitrary")),
        name="modulation",
    )(cc, ada_w, ada_b.reshape(depth, 1, m))


def _inproj_kernel(*refs, has_res):
    if has_res:
        x_ref, y_ref, g2_ref, ng_ref, sh_ref, sc_ref, w_ref, z_ref, xr_ref = refs
        x = x_ref[...] + g2_ref[...] * y_ref[...]
        xr_ref[...] = x
    else:
        x_ref, ng_ref, sh_ref, sc_ref, w_ref, z_ref = refs
        x = x_ref[...]
    ms = jnp.mean(x * x, axis=-1, keepdims=True)
    h = x * lax.rsqrt(ms + EPS) * ng_ref[...]
    h = h * (1.0 + sc_ref[...]) + sh_ref[...]
    z_ref[...] = jnp.dot(h.astype(BF16), w_ref[...], preferred_element_type=F32)


def _inproj(x, res, norm_g, shift, scale, w_in):
    n, d = x.shape
    dz = w_in.shape[1]
    tn = min(n, 512)
    row = pl.BlockSpec((tn, d), lambda i: (i, 0))
    vec = pl.BlockSpec((1, d), lambda i: (0, 0))
    wsp = pl.BlockSpec((d, dz), lambda i: (0, 0))
    zsp = pl.BlockSpec((tn, dz), lambda i: (i, 0))
    if res is None:
        z = pl.pallas_call(
            functools.partial(_inproj_kernel, has_res=False),
            grid=(n // tn,),
            in_specs=[row, vec, vec, vec, wsp],
            out_specs=zsp,
            out_shape=jax.ShapeDtypeStruct((n, dz), F32),
            compiler_params=_cparams(("arbitrary",)),
            name="inproj",
        )(x, norm_g, shift, scale, w_in)
        return z, x
    y, g2 = res
    z, xr = pl.pallas_call(
        functools.partial(_inproj_kernel, has_res=True),
        grid=(n // tn,),
        in_specs=[row, row, vec, vec, vec, vec, wsp],
        out_specs=[zsp, row],
        out_shape=[jax.ShapeDtypeStruct((n, dz), F32), jax.ShapeDtypeStruct((n, d), F32)],
        compiler_params=_cparams(("arbitrary",)),
        name="inproj_res",
    )(x, y, g2, norm_g, shift, scale, w_in)
    return z, xr


def _local_kernel(cx_ref, cb_ref, cc_ref, u_ref, v_ref, px_ref, pc_ref, nx_ref, nc_ref,
                  cw_ref, g_ref, ws_ref, bs_ref, conv_ref, cmlp_ref, *, tn):
    i = pl.program_id(0)
    last = pl.num_programs(0) - 1
    cx = cc_ref[...] * cx_ref[...]
    prev = (pc_ref[...] * px_ref[...])[SUBLANES - 1:SUBLANES, :]
    nxt = (nc_ref[...] * nx_ref[...])[0:1, :]
    prev = jnp.where(i == 0, 0.0, prev)
    nxt = jnp.where(i == last, 0.0, nxt)
    row = _iota((tn, 1), 0)
    up = jnp.where(row == 0, prev, pltpu.roll(cx, 1, 0))
    dn = jnp.where(row == tn - 1, nxt, pltpu.roll(cx, tn - 1, 0))
    w = cw_ref[...]
    conv_ref[...] = cb_ref[...] * (w[0:1] * up + w[1:2] * cx + w[2:3] * dn)

    v = v_ref[...]
    vn = v * lax.rsqrt(jnp.mean(v * v, axis=-1, keepdims=True) + EPS) * g_ref[...]
    head = _iota((1, GROUP_W), 1) >> 6
    for c in range(tn // CHUNK):
        sl = slice(c * CHUNK, (c + 1) * CHUNK)
        vc = vn[sl]
        acc = bs_ref[...]
        for h in range(GROUP_W // HEAD_DIM):
            vm = jnp.where(head == h, vc, 0.0).astype(BF16)
            acc = acc + jnp.dot(ws_ref[h], vm, preferred_element_type=F32)
        cmlp_ref[sl, :] = u_ref[sl, :] * acc


def _local_mixers(z, conv_w, cmlp_g, cmlp_ws, cmlp_bias):
    n = z.shape[0]
    tn = min(n, 256)
    tb = tn // SUBLANES
    nb8 = n // SUBLANES

    def col(j):
        return pl.BlockSpec((tn, GROUP_W), lambda i: (i, j))

    def prev(j):
        return pl.BlockSpec((SUBLANES, GROUP_W), lambda i: (jnp.maximum(i * tb - 1, 0), j))

    def nxt(j):
        return pl.BlockSpec((SUBLANES, GROUP_W), lambda i: (jnp.minimum((i + 1) * tb, nb8 - 1), j))

    out = pl.BlockSpec((tn, GROUP_W), lambda i: (i, 0))
    return pl.pallas_call(
        functools.partial(_local_kernel, tn=tn),
        grid=(n // tn,),
        in_specs=[col(0), col(1), col(2), col(3), col(4), prev(0), prev(2), nxt(0), nxt(2),
                  pl.BlockSpec((3, GROUP_W), lambda i: (0, 0)),
                  pl.BlockSpec((1, GROUP_W), lambda i: (0, 0)),
                  pl.BlockSpec((4, CHUNK, CHUNK), lambda i: (0, 0, 0)),
                  pl.BlockSpec((CHUNK, GROUP_W), lambda i: (0, 0))],
        out_specs=[out, out],
        out_shape=[jax.ShapeDtypeStruct((n, GROUP_W), F32)] * 2,
        compiler_params=_cparams(("arbitrary",)),
        name="local_mixers",
    )(z, z, z, z, z, z, z, z, z, conv_w, cmlp_g, cmlp_ws, cmlp_bias)


def _attprep_kernel(*refs, rope):
    q_ref, kv_ref, qg_ref, kg_ref = refs[:4]
    if rope:
        cos_ref, sin_ref = refs[4:6]
    qo_ref, kt_ref, vd_ref = refs[-3:]
    q = q_ref[...]
    kv = kv_ref[...]
    k = kv[:, :LANES]
    v = kv[:, LANES:]
    same_head = (_iota((LANES, LANES), 0) >> 6) == (_iota((LANES, LANES), 1) >> 6)
    bd = jnp.where(same_head, 1.0 / HEAD_DIM, 0.0)
    lane = _iota((1, LANES), 1)
    first_half = (lane & 31) < 16

    def head_norm(x, g):
        ms = _dot_f32_lhs(x * x, bd)
        return x * lax.rsqrt(ms + EPS) * g

    def rotary(x):
        if not rope:
            return x
        rot = jnp.where(first_half, -pltpu.roll(x, LANES - 16, 1), pltpu.roll(x, 16, 1))
        return x * cos_ref[...] + rot * sin_ref[...]

    for half in range(2):
        sl = slice(half * LANES, (half + 1) * LANES)
        qh = rotary(head_norm(q[:, sl], qg_ref[...]))
        qo_ref[:, sl] = (qh * (HEAD_DIM ** -0.5 * LOG2E)).astype(BF16)
    kn = rotary(head_norm(k, kg_ref[...]))
    kt = kn.T.astype(BF16)
    kt_ref[0:64, :] = kt[0:64]
    kt_ref[64:128, :] = kt[0:64]
    kt_ref[128:192, :] = kt[64:128]
    kt_ref[192:256, :] = kt[64:128]
    vr = pltpu.roll(v, HEAD_DIM, 1)
    low = lane < HEAD_DIM
    vd_ref[:, 0:LANES] = jnp.where(low, v, vr).astype(BF16)
    vd_ref[:, LANES:] = jnp.where(low, vr, v).astype(BF16)


def _attprep(z, q_g, k_g, tables):
    n = z.shape[0]
    tn = min(n, 256)
    rope = tables is not None
    in_specs = [pl.BlockSpec((tn, GROUP_W), lambda i: (i, 5)),
                pl.BlockSpec((tn, GROUP_W), lambda i: (i, 6)),
                pl.BlockSpec((1, LANES), lambda i: (0, 0)),
                pl.BlockSpec((1, LANES), lambda i: (0, 0))]
    args = [z, z, q_g, k_g]
    if rope:
        in_specs += [pl.BlockSpec((tn, LANES), lambda i: (i, 0))] * 2
        args += list(tables)
    return pl.pallas_call(
        functools.partial(_attprep_kernel, rope=rope),
        grid=(n // tn,),
        in_specs=in_specs,
        out_specs=[pl.BlockSpec((tn, GROUP_W), lambda i: (i, 0)),
                   pl.BlockSpec((GROUP_W, tn), lambda i: (0, i)),
                   pl.BlockSpec((tn, GROUP_W), lambda i: (i, 0))],
        out_shape=[jax.ShapeDtypeStruct((n, GROUP_W), BF16),
                   jax.ShapeDtypeStruct((GROUP_W, n), BF16),
                   jax.ShapeDtypeStruct((n, GROUP_W), BF16)],
        compiler_params=_cparams(("arbitrary",)),
        name="attprep_rope" if rope else "attprep",
    )(*args)


def _flash_kernel(q_ref, k_ref, v_ref, o_ref, s_ref, mx_ref, *, tk, nk):
    q = q_ref[...]
    tq = q.shape[0]
    low = _iota((1, LANES), 1) < HEAD_DIM
    zero = jnp.zeros_like(q)
    qa = jnp.where(low, q, zero)
    qb = jnp.where(low, zero, q)

    def scores(j, slot):
        k = k_ref[:, pl.ds(pl.multiple_of(j * tk, LANES), tk)]
        for h, qh in enumerate((qa, qb)):
            s = jnp.dot(qh, k, preferred_element_type=F32)
            s_ref[slot, h] = s
            mx_ref[slot, h] = jnp.max(s, axis=1, keepdims=True)

    def consume(j, slot, carry):
        m_a, acc_a, m_b, acc_b = carry
        v = v_ref[pl.ds(pl.multiple_of(j * tk, LANES), tk), :]
        ones = jnp.ones_like(v)
        n_a = jnp.maximum(m_a, mx_ref[slot, 0])
        n_b = jnp.maximum(m_b, mx_ref[slot, 1])
        p_a = jnp.exp2(s_ref[slot, 0] - n_a).astype(BF16)
        p_b = jnp.exp2(s_ref[slot, 1] - n_b).astype(BF16)
        acc_a = acc_a * jnp.exp2(m_a - n_a) + jnp.dot(p_a, jnp.where(low, v, ones), preferred_element_type=F32)
        acc_b = acc_b * jnp.exp2(m_b - n_b) + jnp.dot(p_b, jnp.where(low, ones, v), preferred_element_type=F32)
        return n_a, acc_a, n_b, acc_b

    neg = jnp.full((tq, 1), -jnp.inf, F32)
    za = jnp.zeros((tq, LANES), F32)
    carry = (neg, za, neg, za)
    scores(0, 0)
    if nk % 2 == 0 and nk >= 4:
        def pair(i, carry):
            j = 2 * i
            scores(j + 1, 1)
            carry = consume(j, 0, carry)
            scores(j + 2, 0)
            return consume(j + 1, 1, carry)

        carry = lax.fori_loop(0, nk // 2 - 1, pair, carry)
        scores(nk - 1, 1)
        carry = consume(nk - 2, 0, carry)
        carry = consume(nk - 1, 1, carry)
    else:
        for j in range(nk):
            if j + 1 < nk:
                scores(j + 1, (j + 1) % 2)
            carry = consume(j, j % 2, carry)
    _, acc_a, _, acc_b = carry
    o_ref[...] = jnp.where(low, acc_a / pltpu.roll(acc_a, HEAD_DIM, 1), acc_b / pltpu.roll(acc_b, HEAD_DIM, 1))


def _kv_tile(nk):
    best = LANES
    for t in range(LANES, FLASH_TK_MAX + 1, LANES):
        if nk % t == 0:
            best = t
    return best


def _flash(q, kt, vd, nk):
    n = q.shape[0]
    tq = min(n, FLASH_TQ)
    tk = _kv_tile(nk)
    return pl.pallas_call(
        functools.partial(_flash_kernel, tk=tk, nk=nk // tk),
        grid=(2, n // tq),
        in_specs=[pl.BlockSpec((tq, LANES), lambda g, i: (i, g)),
                  pl.BlockSpec((LANES, nk), lambda g, i: (g, 0)),
                  pl.BlockSpec((nk, LANES), lambda g, i: (0, g))],
        out_specs=pl.BlockSpec((tq, LANES), lambda g, i: (i, g)),
        out_shape=jax.ShapeDtypeStruct((n, GROUP_W), F32),
        scratch_shapes=[pltpu.VMEM((2, 2, tq, tk), F32), pltpu.VMEM((2, 2, tq, 1), F32)],
        compiler_params=_cparams(("arbitrary", "arbitrary")),
        name="flash_attention",
    )(q, kt, vd)


def _mlstm_chunk(q, k, v, g, h_ref, c_ref, n_ref, m_ref, *, fwd):
    off = 0 if fwd else 4
    nh = GROUP_W // HEAD_DIM
    ks = k * (HEAD_DIM ** -0.5)
    lane = _iota((1, LANES), 1)
    lsig = jnp.minimum(g, 0.0) - jnp.log(1.0 + jnp.exp(-jnp.abs(g)))
    gcol = jnp.where((lane >= 8) & (lane < 16), lsig, g)
    lfcol = pltpu.roll(gcol, LANES - 8, 1)
    rr = _iota((CHUNK, CHUNK), 0)
    cc = _iota((CHUNK, CHUNK), 1)
    sees = (cc <= rr) if fwd else (cc >= rr)
    tri = jnp.where(sees, 1.0, 0.0)
    bc = _dot_f32_rhs(tri, lfcol)
    bct = bc.T
    gt = gcol.T
    total = bc[CHUNK - 1:CHUNK, :] if fwd else bc[0:1, :]
    cp = c_ref[...]
    npv = n_ref[0:1, :]
    mp = m_ref[0:1, :]
    head = _iota((1, GROUP_W), 1) >> 6
    qb = q.astype(BF16)
    kb = ks.astype(BF16)
    qc = lax.dot_general(qb, cp.astype(BF16), NT_DIMS, preferred_element_type=F32)
    heads = range(nh)
    stack = lambda parts: jnp.concatenate(parts, axis=0)
    restack = lambda x: stack([x[:, h * CHUNK:(h + 1) * CHUNK] for h in heads])
    pick = jnp.where(_iota((LANES, nh * CHUNK), 0) == off + (_iota((LANES, nh * CHUNK), 1) >> 7), 1.0, 0.0)
    eh = jnp.where((_iota((GROUP_W, nh * CHUNK), 0) >> 6) == (_iota((GROUP_W, nh * CHUNK), 1) >> 7), 1.0, 0.0)
    nq = restack(jnp.dot((q * npv).astype(BF16), eh.astype(BF16), preferred_element_type=F32))
    spread = _dot_f32_lhs(stack([bc, bc + mp]), pick)
    bc_t = restack(spread[0:CHUNK])
    inter = restack(spread[CHUNK:2 * CHUNK])
    row = stack([jnp.broadcast_to(gt[off + h:off + h + 1, :] - bct[off + h:off + h + 1, :],
                                  (CHUNK, CHUNK)) for h in heads])
    dm = jnp.where(stack([sees] * nh), bc_t + row, -jnp.inf)
    m_t = jnp.maximum(inter, jnp.max(dm, axis=1, keepdims=True))
    q_stack = stack([jnp.where(head == h, q, 0.0) for h in heads]).astype(BF16)
    v_stack = stack([jnp.where(head == h, v, 0.0) for h in heads]).astype(BF16)
    s = lax.dot_general(q_stack, kb, NT_DIMS, preferred_element_type=F32) * jnp.exp(dm - m_t)
    wi = jnp.exp(inter - m_t)
    s_lanes = jnp.concatenate([s[h * CHUNK:(h + 1) * CHUNK] for h in heads], axis=1).astype(BF16)
    num = jnp.dot(s_lanes, v_stack, preferred_element_type=F32)
    den = jnp.sum(s, axis=1, keepdims=True) + wi * nq
    dd = jnp.maximum(jnp.abs(den), jnp.exp(-m_t))
    wi_x = jnp.zeros((CHUNK, GROUP_W), F32)
    den_x = jnp.ones((CHUNK, GROUP_W), F32)
    for h in heads:
        rows = slice(h * CHUNK, (h + 1) * CHUNK)
        wi_x = jnp.where(head == h, jnp.concatenate([wi[rows]] * 2, axis=1), wi_x)
        den_x = jnp.where(head == h, jnp.concatenate([dd[rows]] * 2, axis=1), den_x)
    h_ref[...] = (num + wi_x * qc) / den_x

    a = total - bc + gcol
    m_loc = jnp.max(a, axis=0, keepdims=True)
    w = jnp.exp(a - m_loc)
    m_new = jnp.maximum(total + mp, m_loc)
    d_old = jnp.exp(total + mp - m_new)
    d_loc = jnp.exp(m_loc - m_new)
    pick_w = jnp.where(_iota((LANES, GROUP_W), 0) == off + (_iota((LANES, GROUP_W), 1) >> 6), 1.0, 0.0)
    w_x = _dot_f32_lhs(w, pick_w)
    decay = _dot_f32_lhs(stack([d_old, d_loc, jnp.zeros((SUBLANES - 2, LANES), F32)]), pick_w)
    dold_x = decay[0:1]
    dloc_x = decay[1:2]
    c_loc = jnp.dot((w_x * v).T.astype(BF16), kb, preferred_element_type=F32)
    same_head = (_iota((GROUP_W, GROUP_W), 0) >> 6) == (_iota((GROUP_W, GROUP_W), 1) >> 6)
    c_ref[...] = cp * dold_x + jnp.where(same_head, c_loc, 0.0) * dloc_x
    n_new = npv * dold_x + jnp.sum(w_x * ks, axis=0, keepdims=True) * dloc_x
    n_ref[...] = jnp.broadcast_to(n_new, n_ref.shape)
    m_ref[...] = jnp.broadcast_to(m_new, m_ref.shape)


def _mlstm_kernel(qf_ref, kf_ref, vf_ref, gf_ref, qb_ref, kb_ref, vb_ref, gb_ref, bias_ref,
                  c0f_ref, n0f_ref, m0f_ref, c0b_ref, n0b_ref, m0b_ref,
                  hf_ref, hb_ref, cf_ref, nf_ref, mf_ref, cb_ref, nb_ref, mb_ref):
    @pl.when(pl.program_id(0) == 0)
    def _():
        for dst, src in ((cf_ref, c0f_ref), (nf_ref, n0f_ref), (mf_ref, m0f_ref),
                         (cb_ref, c0b_ref), (nb_ref, n0b_ref), (mb_ref, m0b_ref)):
            dst[...] = src[...]

    _mlstm_chunk(qf_ref[...], kf_ref[...], vf_ref[...], gf_ref[...] + bias_ref[...],
                 hf_ref, cf_ref, nf_ref, mf_ref, fwd=True)
    _mlstm_chunk(qb_ref[...], kb_ref[...], vb_ref[...], gb_ref[...] + bias_ref[...],
                 hb_ref, cb_ref, nb_ref, mb_ref, fwd=False)


def _mlstm(z, gate_b, state_f, state_b):
    n = z.shape[0]
    nc = n // CHUNK

    def cols(order):
        return [pl.BlockSpec((CHUNK, GROUP_W), lambda c, j=j: (order(c), j)) for j in (7, 8, 9)] + [
            pl.BlockSpec((CHUNK, LANES), lambda c: (order(c), N_IN_BLOCKS - 1))]

    def full(shape):
        return pl.BlockSpec(shape, lambda c: (0,) * len(shape))

    fo = lambda c: c
    bo = lambda c: nc - 1 - c
    st_shapes = [(GROUP_W, GROUP_W), (SUBLANES, GROUP_W), (SUBLANES, LANES)]
    st_specs = [full(s) for s in st_shapes]
    outs = pl.pallas_call(
        _mlstm_kernel,
        grid=(nc,),
        in_specs=cols(fo) + cols(bo) + [full((1, LANES))] + st_specs * 2,
        out_specs=[pl.BlockSpec((CHUNK, GROUP_W), lambda c: (fo(c), 0)),
                   pl.BlockSpec((CHUNK, GROUP_W), lambda c: (bo(c), 0))] + st_specs * 2,
        out_shape=[jax.ShapeDtypeStruct((n, GROUP_W), F32)] * 2
        + [jax.ShapeDtypeStruct(s, F32) for s in st_shapes] * 2,
        compiler_params=_cparams(("arbitrary",)),
        name="mlstm",
    )(z, z, z, z, z, z, z, z, gate_b, *state_f, *state_b)
    return outs[0], outs[1], tuple(outs[2:5]), tuple(outs[5:8])


def _outproj_kernel(cv_ref, cm_ref, at_ref, hf_ref, hb_ref, po_ref, mg_ref, w_ref, x_ref, g1_ref, ng_ref,
                    sh_ref, sc_ref, rw_ref, xo_ref, h_ref, aff_ref, *, n_exp):
    hs = hf_ref[...] + hb_ref[...]
    same_head = (_iota((GROUP_W, GROUP_W), 0) >> 6) == (_iota((GROUP_W, GROUP_W), 1) >> 6)
    hms = _dot_f32_lhs(hs * hs, jnp.where(same_head, 1.0 / HEAD_DIM, 0.0))
    ml = _sigmoid(po_ref[...]) * (hs * lax.rsqrt(hms + EPS) * mg_ref[...])
    mix = jnp.zeros(x_ref.shape, F32)
    for j, val in enumerate((cv_ref[...], cm_ref[...], at_ref[...], ml)):
        mix = mix + jnp.dot(val.astype(BF16), w_ref[j * GROUP_W:(j + 1) * GROUP_W, :],
                            preferred_element_type=F32)
    x = x_ref[...] + g1_ref[...] * mix
    xo_ref[...] = x
    ms = jnp.mean(x * x, axis=-1, keepdims=True)
    h = x * lax.rsqrt(ms + EPS) * ng_ref[...]
    h = h * (1.0 + sc_ref[...]) + sh_ref[...]
    h_ref[...] = h.astype(BF16)
    h_hi, h_lo = _split(h, 2)
    w_hi, w_lo = _split(rw_ref[...], 2)
    logits = (jnp.dot(h_hi, w_hi, preferred_element_type=F32) + jnp.dot(h_lo, w_hi, preferred_element_type=F32)
              + jnp.dot(h_hi, w_lo, preferred_element_type=F32))
    lt = logits.T[0:n_exp, :]
    e = jnp.exp(lt - jnp.max(lt, axis=0, keepdims=True))
    aff_ref[...] = e / jnp.sum(e, axis=0, keepdims=True)


def _outproj(mixers, z, ml_norm_g, w_out, x, gate1, norm_g, shift, scale, router_pad, n_exp):
    n, d = x.shape
    tn = min(n, 512)
    mix_spec = pl.BlockSpec((tn, GROUP_W), lambda i: (i, 0))
    row = pl.BlockSpec((tn, d), lambda i: (i, 0))
    vec = pl.BlockSpec((1, d), lambda i: (0, 0))
    return pl.pallas_call(
        functools.partial(_outproj_kernel, n_exp=n_exp),
        grid=(n // tn,),
        in_specs=[mix_spec] * 5 + [pl.BlockSpec((tn, GROUP_W), lambda i: (i, 10)),
                                   pl.BlockSpec((1, GROUP_W), lambda i: (0, 0)),
                                   pl.BlockSpec((d, d), lambda i: (0, 0)), row, vec, vec, vec, vec,
                                   pl.BlockSpec((d, LANES), lambda i: (0, 0))],
        out_specs=[row, row, pl.BlockSpec((n_exp, tn), lambda i: (0, i))],
        out_shape=[jax.ShapeDtypeStruct((n, d), F32), jax.ShapeDtypeStruct((n, d), BF16),
                   jax.ShapeDtypeStruct((n_exp, n), F32)],
        compiler_params=_cparams(("arbitrary",)),
        name="outproj_router",
    )(*mixers, z, ml_norm_g, w_out, x, gate1, norm_g, shift, scale, router_pad)


def _route_kernel(aff_ref, rank_ref, offs_ref, *, cap, iters):
    aff = aff_ref[...]
    n_exp = aff.shape[0]

    def count(mask):
        c = jnp.where(mask, 1.0, 0.0)
        return jnp.sum(jnp.sum(c, axis=1, keepdims=True), axis=2, keepdims=True)

    def bisect(_, carry):
        lo, hi = carry
        mid = 0.5 * (lo + hi)
        ok = count(aff >= mid) >= cap
        return jnp.where(ok, mid, lo), jnp.where(ok, hi, mid)

    _, hi = lax.fori_loop(0, iters, bisect,
                          (jnp.zeros((n_exp, 1, 1), F32), jnp.full((n_exp, 1, 1), 2.0, F32)))
    below = jnp.where(aff < hi, aff, -1.0)
    thr = jnp.max(jnp.max(below, axis=1, keepdims=True), axis=2, keepdims=True)
    gt = aff > thr
    eq = aff == thr
    need = cap - count(gt)

    rr = _iota((LANES, LANES), 0)
    cc = _iota((LANES, LANES), 1)
    upper = jnp.where(rr <= cc, 1.0, 0.0).astype(BF16)
    lower = jnp.where(cc < rr, 1.0, 0.0).astype(BF16)

    def prefix(x01):
        within = jnp.dot(x01.astype(BF16), upper, preferred_element_type=F32)
        offs = jnp.dot(lower, within.astype(BF16), preferred_element_type=F32)[:, LANES - 1:LANES]
        return within - x01 + offs, offs

    offs_all = jnp.zeros((ROUTE_BLOCKS, LANES), F32)
    for e in range(n_exp):
        gt_e = jnp.where(gt[e], 1.0, 0.0)
        eq_e = jnp.where(eq[e], 1.0, 0.0)
        eq_rank, _ = prefix(eq_e)
        sel = jnp.maximum(gt_e, jnp.where(eq_rank < need[e], eq_e, 0.0))
        rk, offs = prefix(sel)
        rk = jnp.where(sel > 0.0, rk, -1.0)
        rank_ref[e] = rk.astype(jnp.int32)
        taken = jnp.sum(jnp.sum(sel, axis=0, keepdims=True), axis=1, keepdims=True)
        offs_all = jnp.where(cc == e, offs, offs_all)
        offs_all = jnp.where(cc == n_exp + e, taken, offs_all)
    offs_ref[...] = offs_all.astype(jnp.int32)


def _route(aff_t, cap):
    n_exp, n = aff_t.shape
    assert n % LANES == 0 and n <= ROUTE_BLOCKS * LANES and cap <= n
    a = aff_t.reshape(n_exp, n // LANES, LANES)
    if n < ROUTE_BLOCKS * LANES:
        a = jnp.pad(a, ((0, 0), (0, ROUTE_BLOCKS - n // LANES), (0, 0)), constant_values=-1.0)
    blk = pl.BlockSpec((n_exp, ROUTE_BLOCKS, LANES), lambda i: (0, 0, 0))
    rank, offs = pl.pallas_call(
        functools.partial(_route_kernel, cap=cap, iters=ROUTE_ITERS),
        grid=(1,),
        in_specs=[blk],
        out_specs=[blk, pl.BlockSpec((ROUTE_BLOCKS, LANES), lambda i: (0, 0))],
        out_shape=[jax.ShapeDtypeStruct((n_exp, ROUTE_BLOCKS, LANES), jnp.int32),
                   jax.ShapeDtypeStruct((ROUTE_BLOCKS, LANES), jnp.int32)],
        compiler_params=_cparams(("arbitrary",)),
        name="route",
    )(a)
    offs_flat = jnp.concatenate([offs[:, :n_exp].T.reshape(-1), offs[0, n_exp:2 * n_exp]])
    return a, rank, offs_flat


def _gather_kernel(offs_ref, h_ref, rank_ref, gate_ref, xs_ref, gs_ref, idx_ref, *, cap, spb):
    e = pl.program_id(0)
    sb = pl.program_id(1)
    outs = (xs_ref, gs_ref, idx_ref)

    taken = offs_ref[pl.num_programs(0) * ROUTE_BLOCKS + e]

    @pl.when((sb == 0) & (taken == cap))
    def _():
        for ref in outs:
            ref[0, 0:GATHER_ALIGN, :] = jnp.zeros((GATHER_ALIGN, ref.shape[2]), ref.dtype)
            ref[0, cap:cap + GATHER_WIN, :] = jnp.zeros((GATHER_WIN, ref.shape[2]), ref.dtype)

    @pl.when((sb == 0) & (taken < cap))
    def _():
        for ref in outs:
            ref[...] = jnp.zeros(ref.shape, ref.dtype)

    lane = _iota((1, LANES), 1)

    def place(k, b, r0a, win):
        rows = _iota((win, LANES), 0)
        rrow = rank_ref[0, pl.ds(b, 1), :]
        grow = gate_ref[0, pl.ds(b, 1), :]
        trow = (b * LANES + lane).astype(F32)
        hit = (rrow - r0a) == rows
        hblk = h_ref[pl.ds(pl.multiple_of(k * LANES, LANES), LANES), :]
        onehot = jnp.where(hit, 1.0, 0.0).astype(BF16)
        rows_x = jnp.dot(onehot, hblk, preferred_element_type=F32).astype(BF16)
        rows_g = jnp.sum(jnp.where(hit, grow, 0.0), axis=1, keepdims=True)
        rows_i = jnp.sum(jnp.where(hit, trow, 0.0), axis=1, keepdims=True)
        head = pl.ds(r0a, GATHER_ALIGN)
        tail = pl.ds(r0a + GATHER_ALIGN, win - GATHER_ALIGN)
        for ref, val in zip(outs, (rows_x, rows_g, rows_i)):
            ref[0, head, :] += val[0:GATHER_ALIGN]
            ref[0, tail, :] = val[GATHER_ALIGN:win]

    group = min(spb, GATHER_GROUP)

    def body(i, carry):
        ks = [i * group + u for u in range(group)]
        bs = [sb * spb + k for k in ks]
        r0as, short = [], None
        for b in bs:
            r0 = offs_ref[e * ROUTE_BLOCKS + b]
            r0a = pl.multiple_of((r0 >> 4) << 4, GATHER_ALIGN)
            nxt = offs_ref[jnp.where(b == ROUTE_BLOCKS - 1, pl.num_programs(0) * ROUTE_BLOCKS + e,
                                     e * ROUTE_BLOCKS + b + 1)]
            fits = nxt - r0a <= GATHER_WIN_SHORT - GATHER_ALIGN
            short = fits if short is None else short & fits
            r0as.append(r0a)

        def run(win):
            for k, b, r0a in zip(ks, bs, r0as):
                place(k, b, r0a, win)

        lax.cond(short, lambda: run(GATHER_WIN_SHORT), lambda: run(GATHER_WIN))
        return carry

    lax.fori_loop(0, spb // group, body, 0)


def _gather(offs_flat, h2, rank, gate, cap):
    n, d = h2.shape
    n_exp = rank.shape[0]
    sbt = min(n, 2048)
    spb = sbt // LANES
    cap_p = cap + GATHER_WIN
    blk = pl.BlockSpec((1, ROUTE_BLOCKS, LANES), lambda e, s, offs: (e, 0, 0))
    out = lambda w: pl.BlockSpec((1, cap_p, w), lambda e, s, offs: (e, 0, 0))
    return pl.pallas_call(
        functools.partial(_gather_kernel, cap=cap, spb=spb),
        grid_spec=pltpu.PrefetchScalarGridSpec(
            num_scalar_prefetch=1,
            grid=(n_exp, n // sbt),
            in_specs=[pl.BlockSpec((sbt, d), lambda e, s, offs: (s, 0)), blk, blk],
            out_specs=[out(d), out(1), out(1)]),
        out_shape=[jax.ShapeDtypeStruct((n_exp, cap_p, d), BF16),
                   jax.ShapeDtypeStruct((n_exp, cap_p, 1), F32),
                   jax.ShapeDtypeStruct((n_exp, cap_p, 1), F32)],
        compiler_params=_cparams(("arbitrary", "arbitrary")),
        name="moe_gather",
    )(offs_flat, h2, rank, gate)


def _ffn_tiles(caps, tm):
    tiles = [[(0, a, min(tm, caps[0] - a))] for a in range(0, caps[0], tm)]
    for s in range(1, len(caps)):
        if caps[s] <= FFN_RIDER_ROWS:
            tiles[-1].append((s, 0, caps[s]))
        else:
            tiles += [[(s, a, min(tm, caps[s] - a))] for a in range(0, caps[s], tm)]
    return tiles


def _ffn_kernel(*refs, caps, tm):
    ns = len(caps)
    xs_refs, (w1_ref, w3_ref, w2_ref) = refs[:ns], refs[ns:ns + 3]
    gs_refs, o_refs = refs[ns + 3:2 * ns + 3], refs[2 * ns + 3:]
    f = pl.program_id(1)

    @pl.when(f == 0)
    def _():
        for o_ref in o_refs:
            o_ref[...] = jnp.zeros(o_ref.shape, F32)

    w1 = w1_ref[0, 0].astype(BF16)
    w3 = w3_ref[0, 0].astype(BF16)
    w2 = w2_ref[0, 0].astype(BF16)
    for tile in _ffn_tiles(caps, tm):
        parts = [xs_refs[s][0, a:a + n, :] for s, a, n in tile]
        x = parts[0] if len(parts) == 1 else jnp.concatenate(parts, axis=0)
        a_ = jnp.dot(x, w1, preferred_element_type=F32)
        b_ = jnp.dot(x, w3, preferred_element_type=F32)
        hid = (a_ * _sigmoid(a_) * b_).astype(BF16)
        y = jnp.dot(hid, w2, preferred_element_type=F32)
        r = 0
        for s, a, n in tile:
            o_refs[s][0, a:a + n, :] += y[r:r + n]
            r += n

    @pl.when(f == pl.num_programs(1) - 1)
    def _():
        for o_ref, gs_ref in zip(o_refs, gs_refs):
            o_ref[0] = o_ref[0] * gs_ref[0]


def _ffn(row_sets, w1, w3, w2, layer):
    caps = tuple(cap for _, _, cap in row_sets)
    n_exp, _, d = row_sets[0][0].shape
    ff = w1.shape[3]
    tf = 512
    tm = min(caps[0], 512)
    rows = lambda cap, w: pl.BlockSpec((1, cap, w), lambda e, f: (e, 0, 0))
    outs = pl.pallas_call(
        functools.partial(_ffn_kernel, caps=caps, tm=tm),
        grid=(n_exp, ff // tf),
        in_specs=[rows(c, d) for c in caps]
        + [pl.BlockSpec((1, 1, d, tf), lambda e, f: (layer, e, 0, f)),
           pl.BlockSpec((1, 1, d, tf), lambda e, f: (layer, e, 0, f)),
           pl.BlockSpec((1, 1, tf, d), lambda e, f: (layer, e, f, 0))]
        + [rows(c, 1) for c in caps],
        out_specs=[rows(c, d) for c in caps],
        out_shape=[jax.ShapeDtypeStruct((n_exp, c, d), F32) for c in caps],
        compiler_params=_cparams(("arbitrary", "arbitrary")),
        name="moe_ffn",
    )(*[xs for xs, _, _ in row_sets], w1, w3, w2, *[gs for _, gs, _ in row_sets])
    return outs


def _combine_kernel(offs_ref, idx_ref, out_ref, y_ref, *, cap, tht):
    th = pl.program_id(0)
    e = pl.program_id(1)

    @pl.when(e == 0)
    def _():
        y_ref[...] = jnp.zeros(y_ref.shape, F32)

    bpt = tht // LANES
    first = e * ROUTE_BLOCKS + th * bpt
    start = offs_ref[first]
    is_last = th == pl.num_programs(0) - 1
    taken = offs_ref[pl.num_programs(1) * ROUTE_BLOCKS + e]
    end = jnp.where(is_last, taken, offs_ref[jnp.minimum(first + bpt, (e + 1) * ROUTE_BLOCKS - 1)])
    base = th * tht

    def token(r):
        return jnp.clip(idx_ref[e * cap + r] - base, 0, tht - 1)

    def group(i, carry):
        r = start + COMBINE_GROUP * i
        ts = [token(r + u) for u in range(COMBINE_GROUP)]
        sums = [y_ref[pl.ds(ts[u], 1), :] + out_ref[0, pl.ds(r + u, 1), :] for u in range(COMBINE_GROUP)]
        for u in range(COMBINE_GROUP):
            y_ref[pl.ds(ts[u], 1), :] = sums[u]
        return carry

    def single(r, carry):
        y_ref[pl.ds(token(r), 1), :] += out_ref[0, pl.ds(r, 1), :]
        return carry

    groups = lax.div(end - start, jnp.int32(COMBINE_GROUP))
    lax.fori_loop(0, groups, group, 0)
    lax.fori_loop(start + groups * COMBINE_GROUP, end, single, 0)


def _combine(offs_flat, idx_flat, out, n):
    n_exp, cap, d = out.shape
    tht = min(n, 4096)
    return pl.pallas_call(
        functools.partial(_combine_kernel, cap=cap, tht=tht),
        grid_spec=pltpu.PrefetchScalarGridSpec(
            num_scalar_prefetch=2,
            grid=(n // tht, n_exp),
            in_specs=[pl.BlockSpec((1, cap, d), lambda t, e, offs, idx: (e, 0, 0))],
            out_specs=pl.BlockSpec((tht, d), lambda t, e, offs, idx: (t, 0))),
        out_shape=jax.ShapeDtypeStruct((n, d), F32),
        compiler_params=_cparams(("arbitrary", "arbitrary")),
        name="moe_combine",
    )(offs_flat, idx_flat, out)


def _moe(routed_sets, w1, w3, w2, layer):
    plans = []
    for h2, aff_t in routed_sets:
        n = h2.shape[0]
        cap = CAPACITY_FACTOR * n // aff_t.shape[0]
        gate, rank, offs_flat = _route(aff_t, cap)
        xs, gs, idx = _gather(offs_flat, h2, rank, gate, cap)
        plans.append((offs_flat, idx[:, :cap, 0].astype(jnp.int32).reshape(-1), xs, gs, cap, n))
    outs = _ffn([(p[2], p[3], p[4]) for p in plans], w1, w3, w2, layer)
    return [_combine(p[0], p[1], out, p[5]) for p, out in zip(plans, outs)]


def _final_kernel(x_ref, y_ref, g2_ref, ng_ref, o_ref):
    x = x_ref[...] + g2_ref[...] * y_ref[...]
    ms = jnp.mean(x * x, axis=-1, keepdims=True)
    o_ref[...] = x * lax.rsqrt(ms + EPS) * ng_ref[...]


def _final(x, y, g2, norm_g):
    n, d = x.shape
    tn = min(n, 512)
    row = pl.BlockSpec((tn, d), lambda i: (i, 0))
    vec = pl.BlockSpec((1, d), lambda i: (0, 0))
    return pl.pallas_call(
        _final_kernel,
        grid=(n // tn,),
        in_specs=[row, row, vec, vec],
        out_specs=row,
        out_shape=jax.ShapeDtypeStruct((n, d), F32),
        compiler_params=_cparams(("arbitrary",)),
        name="final_norm",
    )(x, y, g2, norm_g)


def _rope_tables(n):
    pos = jnp.arange(n)
    row, colp = pos // GRID_W, pos % GRID_W
    axis_dim = HEAD_DIM // 2
    inv_freq = ROPE_THETA ** (-jnp.arange(0, axis_dim, 2, dtype=F32) / axis_dim)

    def axis_angles(p):
        a = p.astype(F32)[:, None] * inv_freq[None, :]
        return jnp.concatenate([a, a], axis=-1)

    ang = jnp.concatenate([axis_angles(row), axis_angles(colp)], axis=-1)
    ang = jnp.concatenate([ang, ang], axis=-1)
    return jnp.cos(ang), jnp.sin(ang)


def _zero_state():
    return (jnp.zeros((GROUP_W, GROUP_W), F32), jnp.zeros((SUBLANES, GROUP_W), F32),
            jnp.zeros((SUBLANES, LANES), F32))


def kernel(x, c, ctx, c_ctx, ada_w, ada_b, norm1_g, w_in, conv_w, cmlp_norm_g, cmlp_ws, cmlp_bs, q_norm_g, k_norm_g, ml_igate_b, ml_fgate_b, ml_norm_g, w_out, norm2_g, router_w, exp_w1, exp_w3, exp_w2, final_norm_g):
    assert x.shape[0] == 1 and ctx.shape[0] == 1
    depth, d, d_in = w_in.shape
    n_exp = router_w.shape[-1]
    assert d_in <= N_IN_BLOCKS * LANES and d == 4 * GROUP_W
    xs = x[0]
    xc = ctx[0]
    n = xs.shape[0]

    mod = _modulation(jnp.stack([c[0], c_ctx], axis=1), ada_w, ada_b)
    tables = _rope_tables(n)
    row = lambda v: v.reshape(1, -1)

    res_x = None
    res_c = None
    for l in range(depth):
        parts = [[mod[l, s:s + 1, j * d:(j + 1) * d] for j in range(6)] for s in range(2)]
        sh1, sc1, g1, sh2, sc2, g2 = parts[0]
        csh1, csc1, cg1, csh2, csc2, cg2 = parts[1]
        w_in_l = jnp.pad(w_in[l], ((0, 0), (0, N_IN_BLOCKS * LANES - d_in))).astype(BF16)
        w_out_l = w_out[l].astype(BF16)
        cmlp_ws_l = cmlp_ws[l].astype(BF16)
        cmlp_bias = jnp.repeat(cmlp_bs[l].T, HEAD_DIM, axis=1)
        qg = jnp.tile(q_norm_g[l], 2).reshape(1, LANES)
        kg = jnp.tile(k_norm_g[l], 2).reshape(1, LANES)
        gate_b = jnp.pad(jnp.concatenate([ml_igate_b[l], ml_fgate_b[l]]), (0, LANES - 16)).reshape(1, LANES)
        router_pad = jnp.pad(router_w[l], ((0, 0), (0, LANES - n_exp)))
        update_ctx = l < depth - 1

        zc, xc = _inproj(xc, res_c, row(norm1_g[l]), csh1, csc1, w_in_l)
        n_ctx = xc.shape[0]
        qc_, ktc, vdc = _attprep(zc, qg, kg, None)
        hcf, hcb, st_f, st_b = _mlstm(zc, gate_b, _zero_state(), _zero_state())
        if update_ctx:
            conv_c, cmlp_c = _local_mixers(zc, conv_w[l], row(cmlp_norm_g[l]), cmlp_ws_l, cmlp_bias)
            att_c = _flash(qc_, ktc, vdc, n_ctx)
            xc, hc2, affc = _outproj((conv_c, cmlp_c, att_c, hcf, hcb), zc, row(ml_norm_g[l]), w_out_l, xc, cg1,
                                     row(norm2_g[l]), csh2, csc2, router_pad, n_exp)

        z, xs = _inproj(xs, res_x, row(norm1_g[l]), sh1, sc1, w_in_l)
        conv_x, cmlp_x = _local_mixers(z, conv_w[l], row(cmlp_norm_g[l]), cmlp_ws_l, cmlp_bias)
        q_, kt, vd = _attprep(z, qg, kg, tables)
        att_x = _flash(q_, jnp.concatenate([ktc, kt], axis=1), jnp.concatenate([vdc, vd], axis=0), n_ctx + n)
        hf, hb, _, _ = _mlstm(z, gate_b, st_f, st_b)
        xs, h2, aff = _outproj((conv_x, cmlp_x, att_x, hf, hb), z, row(ml_norm_g[l]), w_out_l, xs, g1,
                               row(norm2_g[l]), sh2, sc2, router_pad, n_exp)
        routed = [(h2, aff)] + ([(hc2, affc)] if update_ctx else [])
        ys = _moe(routed, exp_w1, exp_w3, exp_w2, l)
        res_x = (ys[0], g2)
        if update_ctx:
            res_c = (ys[1], cg2)

    return _final(xs, res_x[0], res_x[1], row(final_norm_g))[None]
```

```python
import functools

import jax
import jax.numpy as jnp
from jax import lax
from jax.experimental import pallas as pl
from jax.experimental.pallas import tpu as pltpu

F32 = jnp.float32
BF16 = jnp.bfloat16
HIGHEST = lax.Precision.HIGHEST

EPS = 1e-6
LOG2E = 1.4426950408889634
GRID_W = 64
ROPE_THETA = 10000.0
CAPACITY_FACTOR = 2
GROUP_W = 256
HEAD_DIM = 64
CHUNK = 128
N_IN_BLOCKS = 23

LANES = 128
SUBLANES = 8
VMEM_LIMIT = 56 * 1024 * 1024

ROUTE_BLOCKS = 128
ROUTE_ITERS = 48
GATHER_ALIGN = 16
GATHER_WIN = 160
GATHER_WIN_SHORT = 64
GATHER_GROUP = 4
COMBINE_GROUP = 4
FFN_RIDER_ROWS = 64

MOD_TILE = 1536
ROW_TILE = 512
MIXER_TILE = 256
GATHER_TOKENS = 2048
FFN_TF = 512
FFN_TM = 512
COMBINE_TOKENS = 4096
FLASH_TQ = 512
FLASH_TK_MAX = 3328

NT_DIMS = (((1,), (1,)), ((), ()))


def _cparams(sem, vmem=None):
    return pltpu.CompilerParams(dimension_semantics=sem, vmem_limit_bytes=vmem or VMEM_LIMIT)


def _sigmoid(x):
    return 1.0 / (1.0 + jnp.exp(-x))


def _iota(shape, dim):
    return lax.broadcasted_iota(jnp.int32, shape, dim)


def _split(x, pieces):
    out = []
    for _ in range(pieces - 1):
        p = x.astype(BF16)
        out.append(p)
        x = x - p.astype(F32)
    return out + [x.astype(BF16)]


def _dot_f32_lhs(x, w):
    w = w.astype(BF16)
    return sum(jnp.dot(p, w, preferred_element_type=F32) for p in _split(x, 3))


def _dot_f32_rhs(w, x):
    w = w.astype(BF16)
    return sum(jnp.dot(w, p, preferred_element_type=F32) for p in _split(x, 3))


def _mod_kernel(cc_ref, w_ref, b_ref, o_ref):
    cc = cc_ref[...]
    s = cc * _sigmoid(cc)
    w = w_ref[0]
    b = b_ref[0]
    o_ref[0, 0:1, :] = jnp.sum(w * s[:, 0:1], axis=0, keepdims=True) + b
    o_ref[0, 1:2, :] = jnp.sum(w * s[:, 1:2], axis=0, keepdims=True) + b


def _modulation(cc, ada_w, ada_b):
    depth, d, m = ada_w.shape
    tn = MOD_TILE
    return pl.pallas_call(
        _mod_kernel,
        grid=(depth, m // tn),
        in_specs=[pl.BlockSpec((d, 2), lambda l, j: (0, 0)),
                  pl.BlockSpec((1, d, tn), lambda l, j: (l, 0, j)),
                  pl.BlockSpec((1, 1, tn), lambda l, j: (l, 0, j))],
        out_specs=pl.BlockSpec((1, 2, tn), lambda l, j: (l, 0, j)),
        out_shape=jax.ShapeDtypeStruct((depth, 2, m), F32),
        compiler_params=_cparams(("arbitrary", "arbitrary")),
        name="modulation",
    )(cc, ada_w, ada_b.reshape(depth, 1, m))


def _inproj_kernel(*refs, has_res):
    if has_res:
        x_ref, y_ref, g2_ref, ng_ref, sh_ref, sc_ref, w_ref, z_ref, xr_ref = refs
        x = x_ref[...] + g2_ref[...] * y_ref[...]
        xr_ref[...] = x
    else:
        x_ref, ng_ref, sh_ref, sc_ref, w_ref, z_ref = refs
        x = x_ref[...]
    ms = jnp.mean(x * x, axis=-1, keepdims=True)
    h = x * lax.rsqrt(ms + EPS) * ng_ref[...]
    h = h * (1.0 + sc_ref[...]) + sh_ref[...]
    z_ref[...] = jnp.dot(h.astype(BF16), w_ref[...], preferred_element_type=F32)


def _inproj(x, res, norm_g, shift, scale, w_in):
    n, d = x.shape
    dz = w_in.shape[1]
    tn = min(n, ROW_TILE)
    row = pl.BlockSpec((tn, d), lambda i: (i, 0))
    vec = pl.BlockSpec((1, d), lambda i: (0, 0))
    wsp = pl.BlockSpec((d, dz), lambda i: (0, 0))
    zsp = pl.BlockSpec((tn, dz), lambda i: (i, 0))
    if res is None:
        z = pl.pallas_call(
            functools.partial(_inproj_kernel, has_res=False),
            grid=(n // tn,),
            in_specs=[row, vec, vec, vec, wsp],
            out_specs=zsp,
            out_shape=jax.ShapeDtypeStruct((n, dz), F32),
            compiler_params=_cparams(("arbitrary",)),
            name="inproj",
        )(x, norm_g, shift, scale, w_in)
        return z, x
    y, g2 = res
    z, xr = pl.pallas_call(
        functools.partial(_inproj_kernel, has_res=True),
        grid=(n // tn,),
        in_specs=[row, row, vec, vec, vec, vec, wsp],
        out_specs=[zsp, row],
        out_shape=[jax.ShapeDtypeStruct((n, dz), F32), jax.ShapeDtypeStruct((n, d), F32)],
        compiler_params=_cparams(("arbitrary",)),
        name="inproj_res",
    )(x, y, g2, norm_g, shift, scale, w_in)
    return z, xr


def _local_kernel(cx_ref, cb_ref, cc_ref, u_ref, v_ref, px_ref, pc_ref, nx_ref, nc_ref,
                  cw_ref, g_ref, ws_ref, bs_ref, conv_ref, cmlp_ref, *, tn):
    i = pl.program_id(0)
    last = pl.num_programs(0) - 1
    cx = cc_ref[...] * cx_ref[...]
    prev = (pc_ref[...] * px_ref[...])[SUBLANES - 1:SUBLANES, :]
    nxt = (nc_ref[...] * nx_ref[...])[0:1, :]
    prev = jnp.where(i == 0, 0.0, prev)
    nxt = jnp.where(i == last, 0.0, nxt)
    row = _iota((tn, 1), 0)
    up = jnp.where(row == 0, prev, pltpu.roll(cx, 1, 0))
    dn = jnp.where(row == tn - 1, nxt, pltpu.roll(cx, tn - 1, 0))
    w = cw_ref[...]
    conv_ref[...] = cb_ref[...] * (w[0:1] * up + w[1:2] * cx + w[2:3] * dn)

    v = v_ref[...]
    vn = v * lax.rsqrt(jnp.mean(v * v, axis=-1, keepdims=True) + EPS) * g_ref[...]
    head = _iota((1, GROUP_W), 1) >> 6
    for c in range(tn // CHUNK):
        sl = slice(c * CHUNK, (c + 1) * CHUNK)
        vc = vn[sl]
        acc = bs_ref[...]
        for h in range(GROUP_W // HEAD_DIM):
            vm = jnp.where(head == h, vc, 0.0).astype(BF16)
            acc = acc + jnp.dot(ws_ref[h], vm, preferred_element_type=F32)
        cmlp_ref[sl, :] = u_ref[sl, :] * acc


def _local_mixers(z, conv_w, cmlp_g, cmlp_ws, cmlp_bias):
    n = z.shape[0]
    tn = min(n, MIXER_TILE)
    tb = tn // SUBLANES
    nb8 = n // SUBLANES

    def col(j):
        return pl.BlockSpec((tn, GROUP_W), lambda i: (i, j))

    def prev(j):
        return pl.BlockSpec((SUBLANES, GROUP_W), lambda i: (jnp.maximum(i * tb - 1, 0), j))

    def nxt(j):
        return pl.BlockSpec((SUBLANES, GROUP_W), lambda i: (jnp.minimum((i + 1) * tb, nb8 - 1), j))

    out = pl.BlockSpec((tn, GROUP_W), lambda i: (i, 0))
    return pl.pallas_call(
        functools.partial(_local_kernel, tn=tn),
        grid=(n // tn,),
        in_specs=[col(0), col(1), col(2), col(3), col(4), prev(0), prev(2), nxt(0), nxt(2),
                  pl.BlockSpec((3, GROUP_W), lambda i: (0, 0)),
                  pl.BlockSpec((1, GROUP_W), lambda i: (0, 0)),
                  pl.BlockSpec((4, CHUNK, CHUNK), lambda i: (0, 0, 0)),
                  pl.BlockSpec((CHUNK, GROUP_W), lambda i: (0, 0))],
        out_specs=[out, out],
        out_shape=[jax.ShapeDtypeStruct((n, GROUP_W), F32)] * 2,
        compiler_params=_cparams(("arbitrary",)),
        name="local_mixers",
    )(z, z, z, z, z, z, z, z, z, conv_w, cmlp_g, cmlp_ws, cmlp_bias)


def _attprep_kernel(*refs, rope):
    q_ref, kv_ref, qg_ref, kg_ref = refs[:4]
    if rope:
        cos_ref, sin_ref = refs[4:6]
    qo_ref, kt_ref, vd_ref = refs[-3:]
    q = q_ref[...]
    kv = kv_ref[...]
    k = kv[:, :LANES]
    v = kv[:, LANES:]
    same_head = (_iota((LANES, LANES), 0) >> 6) == (_iota((LANES, LANES), 1) >> 6)
    bd = jnp.where(same_head, 1.0 / HEAD_DIM, 0.0)
    lane = _iota((1, LANES), 1)
    first_half = (lane & 31) < 16

    def head_norm(x, g):
        ms = _dot_f32_lhs(x * x, bd)
        return x * lax.rsqrt(ms + EPS) * g

    def rotary(x):
        if not rope:
            return x
        rot = jnp.where(first_half, -pltpu.roll(x, LANES - 16, 1), pltpu.roll(x, 16, 1))
        return x * cos_ref[...] + rot * sin_ref[...]

    for half in range(2):
        sl = slice(half * LANES, (half + 1) * LANES)
        qh = rotary(head_norm(q[:, sl], qg_ref[...]))
        qo_ref[:, sl] = (qh * (HEAD_DIM ** -0.5 * LOG2E)).astype(BF16)
    kn = rotary(head_norm(k, kg_ref[...]))
    kt = kn.T.astype(BF16)
    kt_ref[0:64, :] = kt[0:64]
    kt_ref[64:128, :] = kt[0:64]
    kt_ref[128:192, :] = kt[64:128]
    kt_ref[192:256, :] = kt[64:128]
    vr = pltpu.roll(v, HEAD_DIM, 1)
    low = lane < HEAD_DIM
    vd_ref[:, 0:LANES] = jnp.where(low, v, vr).astype(BF16)
    vd_ref[:, LANES:] = jnp.where(low, vr, v).astype(BF16)


def _attprep(z, q_g, k_g, tables):
    n = z.shape[0]
    tn = min(n, MIXER_TILE)
    rope = tables is not None
    in_specs = [pl.BlockSpec((tn, GROUP_W), lambda i: (i, 5)),
                pl.BlockSpec((tn, GROUP_W), lambda i: (i, 6)),
                pl.BlockSpec((1, LANES), lambda i: (0, 0)),
                pl.BlockSpec((1, LANES), lambda i: (0, 0))]
    args = [z, z, q_g, k_g]
    if rope:
        in_specs += [pl.BlockSpec((tn, LANES), lambda i: (i, 0))] * 2
        args += list(tables)
    return pl.pallas_call(
        functools.partial(_attprep_kernel, rope=rope),
        grid=(n // tn,),
        in_specs=in_specs,
        out_specs=[pl.BlockSpec((tn, GROUP_W), lambda i: (i, 0)),
                   pl.BlockSpec((GROUP_W, tn), lambda i: (0, i)),
                   pl.BlockSpec((tn, GROUP_W), lambda i: (i, 0))],
        out_shape=[jax.ShapeDtypeStruct((n, GROUP_W), BF16),
                   jax.ShapeDtypeStruct((GROUP_W, n), BF16),
                   jax.ShapeDtypeStruct((n, GROUP_W), BF16)],
        compiler_params=_cparams(("arbitrary",)),
        name="attprep_rope" if rope else "attprep",
    )(*args)


def _flash_kernel(q_ref, k_ref, v_ref, o_ref, s_ref, mx_ref, *, tk, nk):
    q = q_ref[...]
    tq = q.shape[0]
    low = _iota((1, LANES), 1) < HEAD_DIM
    zero = jnp.zeros_like(q)
    qa = jnp.where(low, q, zero)
    qb = jnp.where(low, zero, q)

    def scores(j, slot):
        k = k_ref[:, pl.ds(pl.multiple_of(j * tk, LANES), tk)]
        for h, qh in enumerate((qa, qb)):
            s = jnp.dot(qh, k, preferred_element_type=F32)
            s_ref[slot, h] = s
            mx_ref[slot, h] = jnp.max(s, axis=1, keepdims=True)

    def consume(j, slot, carry):
        m_a, acc_a, m_b, acc_b = carry
        v = v_ref[pl.ds(pl.multiple_of(j * tk, LANES), tk), :]
        ones = jnp.ones_like(v)
        n_a = jnp.maximum(m_a, mx_ref[slot, 0])
        n_b = jnp.maximum(m_b, mx_ref[slot, 1])
        p_a = jnp.exp2(s_ref[slot, 0] - n_a).astype(BF16)
        p_b = jnp.exp2(s_ref[slot, 1] - n_b).astype(BF16)
        acc_a = acc_a * jnp.exp2(m_a - n_a) + jnp.dot(p_a, jnp.where(low, v, ones), preferred_element_type=F32)
        acc_b = acc_b * jnp.exp2(m_b - n_b) + jnp.dot(p_b, jnp.where(low, ones, v), preferred_element_type=F32)
        return n_a, acc_a, n_b, acc_b

    neg = jnp.full((tq, 1), -jnp.inf, F32)
    za = jnp.zeros((tq, LANES), F32)
    carry = (neg, za, neg, za)
    scores(0, 0)
    if nk % 2 == 0 and nk >= 4:
        def pair(i, carry):
            j = 2 * i
            scores(j + 1, 1)
            carry = consume(j, 0, carry)
            scores(j + 2, 0)
            return consume(j + 1, 1, carry)

        carry = lax.fori_loop(0, nk // 2 - 1, pair, carry)
        scores(nk - 1, 1)
        carry = consume(nk - 2, 0, carry)
        carry = consume(nk - 1, 1, carry)
    else:
        for j in range(nk):
            if j + 1 < nk:
                scores(j + 1, (j + 1) % 2)
            carry = consume(j, j % 2, carry)
    _, acc_a, _, acc_b = carry
    o_ref[...] = jnp.where(low, acc_a / pltpu.roll(acc_a, HEAD_DIM, 1), acc_b / pltpu.roll(acc_b, HEAD_DIM, 1))


def _kv_tile(nk):
    best = LANES
    for t in range(LANES, FLASH_TK_MAX + 1, LANES):
        if nk % t == 0:
            best = t
    return best


def _flash(q, kt, vd, nk):
    n = q.shape[0]
    tq = min(n, FLASH_TQ)
    tk = _kv_tile(nk)
    return pl.pallas_call(
        functools.partial(_flash_kernel, tk=tk, nk=nk // tk),
        grid=(2, n // tq),
        in_specs=[pl.BlockSpec((tq, LANES), lambda g, i: (i, g)),
                  pl.BlockSpec((LANES, nk), lambda g, i: (g, 0)),
                  pl.BlockSpec((nk, LANES), lambda g, i: (0, g))],
        out_specs=pl.BlockSpec((tq, LANES), lambda g, i: (i, g)),
        out_shape=jax.ShapeDtypeStruct((n, GROUP_W), F32),
        scratch_shapes=[pltpu.VMEM((2, 2, tq, tk), F32), pltpu.VMEM((2, 2, tq, 1), F32)],
        compiler_params=_cparams(("arbitrary", "arbitrary")),
        name="flash_attention",
    )(q, kt, vd)


def _mlstm_chunk(q, k, v, g, h_ref, c_ref, n_ref, m_ref, *, fwd):
    off = 0 if fwd else 4
    nh = GROUP_W // HEAD_DIM
    ks = k * (HEAD_DIM ** -0.5)
    lane = _iota((1, LANES), 1)
    lsig = jnp.minimum(g, 0.0) - jnp.log(1.0 + jnp.exp(-jnp.abs(g)))
    gcol = jnp.where((lane >= 8) & (lane < 16), lsig, g)
    lfcol = pltpu.roll(gcol, LANES - 8, 1)
    rr = _iota((CHUNK, CHUNK), 0)
    cc = _iota((CHUNK, CHUNK), 1)
    sees = (cc <= rr) if fwd else (cc >= rr)
    tri = jnp.where(sees, 1.0, 0.0)
    bc = _dot_f32_rhs(tri, lfcol)
    bct = bc.T
    gt = gcol.T
    total = bc[CHUNK - 1:CHUNK, :] if fwd else bc[0:1, :]
    cp = c_ref[...]
    npv = n_ref[0:1, :]
    mp = m_ref[0:1, :]
    head = _iota((1, GROUP_W), 1) >> 6
    qb = q.astype(BF16)
    kb = ks.astype(BF16)
    qc = lax.dot_general(qb, cp.astype(BF16), NT_DIMS, preferred_element_type=F32)
    heads = range(nh)
    stack = lambda parts: jnp.concatenate(parts, axis=0)
    restack = lambda x: stack([x[:, h * CHUNK:(h + 1) * CHUNK] for h in heads])
    pick = jnp.where(_iota((LANES, nh * CHUNK), 0) == off + (_iota((LANES, nh * CHUNK), 1) >> 7), 1.0, 0.0)
    eh = jnp.where((_iota((GROUP_W, nh * CHUNK), 0) >> 6) == (_iota((GROUP_W, nh * CHUNK), 1) >> 7), 1.0, 0.0)
    nq = restack(jnp.dot((q * npv).astype(BF16), eh.astype(BF16), preferred_element_type=F32))
    spread = _dot_f32_lhs(stack([bc, bc + mp]), pick)
    bc_t = restack(spread[0:CHUNK])
    inter = restack(spread[CHUNK:2 * CHUNK])
    row = stack([jnp.broadcast_to(gt[off + h:off + h + 1, :] - bct[off + h:off + h + 1, :],
                                  (CHUNK, CHUNK)) for h in heads])
    dm = jnp.where(stack([sees] * nh), bc_t + row, -jnp.inf)
    m_t = jnp.maximum(inter, jnp.max(dm, axis=1, keepdims=True))
    q_stack = stack([jnp.where(head == h, q, 0.0) for h in heads]).astype(BF16)
    v_stack = stack([jnp.where(head == h, v, 0.0) for h in heads]).astype(BF16)
    s = lax.dot_general(q_stack, kb, NT_DIMS, preferred_element_type=F32) * jnp.exp(dm - m_t)
    wi = jnp.exp(inter - m_t)
    s_lanes = jnp.concatenate([s[h * CHUNK:(h + 1) * CHUNK] for h in heads], axis=1).astype(BF16)
    num = jnp.dot(s_lanes, v_stack, preferred_element_type=F32)
    den = jnp.sum(s, axis=1, keepdims=True) + wi * nq
    dd = jnp.maximum(jnp.abs(den), jnp.exp(-m_t))
    wi_x = jnp.zeros((CHUNK, GROUP_W), F32)
    den_x = jnp.ones((CHUNK, GROUP_W), F32)
    for h in heads:
        rows = slice(h * CHUNK, (h + 1) * CHUNK)
        wi_x = jnp.where(head == h, jnp.concatenate([wi[rows]] * 2, axis=1), wi_x)
        den_x = jnp.where(head == h, jnp.concatenate([dd[rows]] * 2, axis=1), den_x)
    h_ref[...] = (num + wi_x * qc) / den_x

    a = total - bc + gcol
    m_loc = jnp.max(a, axis=0, keepdims=True)
    w = jnp.exp(a - m_loc)
    m_new = jnp.maximum(total + mp, m_loc)
    d_old = jnp.exp(total + mp - m_new)
    d_loc = jnp.exp(m_loc - m_new)
    pick_w = jnp.where(_iota((LANES, GROUP_W), 0) == off + (_iota((LANES, GROUP_W), 1) >> 6), 1.0, 0.0)
    w_x = _dot_f32_lhs(w, pick_w)
    decay = _dot_f32_lhs(stack([d_old, d_loc, jnp.zeros((SUBLANES - 2, LANES), F32)]), pick_w)
    dold_x = decay[0:1]
    dloc_x = decay[1:2]
    c_loc = jnp.dot((w_x * v).T.astype(BF16), kb, preferred_element_type=F32)
    same_head = (_iota((GROUP_W, GROUP_W), 0) >> 6) == (_iota((GROUP_W, GROUP_W), 1) >> 6)
    c_ref[...] = cp * dold_x + jnp.where(same_head, c_loc, 0.0) * dloc_x
    n_new = npv * dold_x + jnp.sum(w_x * ks, axis=0, keepdims=True) * dloc_x
    n_ref[...] = jnp.broadcast_to(n_new, n_ref.shape)
    m_ref[...] = jnp.broadcast_to(m_new, m_ref.shape)


def _mlstm_kernel(qf_ref, kf_ref, vf_ref, gf_ref, qb_ref, kb_ref, vb_ref, gb_ref, bias_ref,
                  c0f_ref, n0f_ref, m0f_ref, c0b_ref, n0b_ref, m0b_ref,
                  hf_ref, hb_ref, cf_ref, nf_ref, mf_ref, cb_ref, nb_ref, mb_ref):
    @pl.when(pl.program_id(0) == 0)
    def _():
        for dst, src in ((cf_ref, c0f_ref), (nf_ref, n0f_ref), (mf_ref, m0f_ref),
                         (cb_ref, c0b_ref), (nb_ref, n0b_ref), (mb_ref, m0b_ref)):
            dst[...] = src[...]

    _mlstm_chunk(qf_ref[...], kf_ref[...], vf_ref[...], gf_ref[...] + bias_ref[...],
                 hf_ref, cf_ref, nf_ref, mf_ref, fwd=True)
    _mlstm_chunk(qb_ref[...], kb_ref[...], vb_ref[...], gb_ref[...] + bias_ref[...],
                 hb_ref, cb_ref, nb_ref, mb_ref, fwd=False)


def _mlstm(z, gate_b, state_f, state_b):
    n = z.shape[0]
    nc = n // CHUNK

    def cols(order):
        return [pl.BlockSpec((CHUNK, GROUP_W), lambda c, j=j: (order(c), j)) for j in (7, 8, 9)] + [
            pl.BlockSpec((CHUNK, LANES), lambda c: (order(c), N_IN_BLOCKS - 1))]

    def full(shape):
        return pl.BlockSpec(shape, lambda c: (0,) * len(shape))

    fo = lambda c: c
    bo = lambda c: nc - 1 - c
    st_shapes = [(GROUP_W, GROUP_W), (SUBLANES, GROUP_W), (SUBLANES, LANES)]
    st_specs = [full(s) for s in st_shapes]
    outs = pl.pallas_call(
        _mlstm_kernel,
        grid=(nc,),
        in_specs=cols(fo) + cols(bo) + [full((1, LANES))] + st_specs * 2,
        out_specs=[pl.BlockSpec((CHUNK, GROUP_W), lambda c: (fo(c), 0)),
                   pl.BlockSpec((CHUNK, GROUP_W), lambda c: (bo(c), 0))] + st_specs * 2,
        out_shape=[jax.ShapeDtypeStruct((n, GROUP_W), F32)] * 2
        + [jax.ShapeDtypeStruct(s, F32) for s in st_shapes] * 2,
        compiler_params=_cparams(("arbitrary",)),
        name="mlstm",
    )(z, z, z, z, z, z, z, z, gate_b, *state_f, *state_b)
    return outs[0], outs[1], tuple(outs[2:5]), tuple(outs[5:8])


def _outproj_kernel(cv_ref, cm_ref, at_ref, hf_ref, hb_ref, po_ref, mg_ref, w_ref, x_ref, g1_ref, ng_ref,
                    sh_ref, sc_ref, rw_ref, xo_ref, h_ref, aff_ref, *, n_exp):
    hs = hf_ref[...] + hb_ref[...]
    same_head = (_iota((GROUP_W, GROUP_W), 0) >> 6) == (_iota((GROUP_W, GROUP_W), 1) >> 6)
    hms = _dot_f32_lhs(hs * hs, jnp.where(same_head, 1.0 / HEAD_DIM, 0.0))
    ml = _sigmoid(po_ref[...]) * (hs * lax.rsqrt(hms + EPS) * mg_ref[...])
    mix = jnp.zeros(x_ref.shape, F32)
    for j, val in enumerate((cv_ref[...], cm_ref[...], at_ref[...], ml)):
        mix = mix + jnp.dot(val.astype(BF16), w_ref[j * GROUP_W:(j + 1) * GROUP_W, :],
                            preferred_element_type=F32)
    x = x_ref[...] + g1_ref[...] * mix
    xo_ref[...] = x
    ms = jnp.mean(x * x, axis=-1, keepdims=True)
    h = x * lax.rsqrt(ms + EPS) * ng_ref[...]
    h = h * (1.0 + sc_ref[...]) + sh_ref[...]
    h_ref[...] = h.astype(BF16)
    h_hi, h_lo = _split(h, 2)
    w_hi, w_lo = _split(rw_ref[...], 2)
    logits = (jnp.dot(h_hi, w_hi, preferred_element_type=F32) + jnp.dot(h_lo, w_hi, preferred_element_type=F32)
              + jnp.dot(h_hi, w_lo, preferred_element_type=F32))
    lt = logits.T[0:n_exp, :]
    e = jnp.exp(lt - jnp.max(lt, axis=0, keepdims=True))
    aff_ref[...] = e / jnp.sum(e, axis=0, keepdims=True)


def _outproj(mixers, z, ml_norm_g, w_out, x, gate1, norm_g, shift, scale, router_pad, n_exp):
    n, d = x.shape
    tn = min(n, ROW_TILE)
    mix_spec = pl.BlockSpec((tn, GROUP_W), lambda i: (i, 0))
    row = pl.BlockSpec((tn, d), lambda i: (i, 0))
    vec = pl.BlockSpec((1, d), lambda i: (0, 0))
    return pl.pallas_call(
        functools.partial(_outproj_kernel, n_exp=n_exp),
        grid=(n // tn,),
        in_specs=[mix_spec] * 5 + [pl.BlockSpec((tn, GROUP_W), lambda i: (i, 10)),
                                   pl.BlockSpec((1, GROUP_W), lambda i: (0, 0)),
                                   pl.BlockSpec((d, d), lambda i: (0, 0)), row, vec, vec, vec, vec,
                                   pl.BlockSpec((d, LANES), lambda i: (0, 0))],
        out_specs=[row, row, pl.BlockSpec((n_exp, tn), lambda i: (0, i))],
        out_shape=[jax.ShapeDtypeStruct((n, d), F32), jax.ShapeDtypeStruct((n, d), BF16),
                   jax.ShapeDtypeStruct((n_exp, n), F32)],
        compiler_params=_cparams(("arbitrary",)),
        name="outproj_router",
    )(*mixers, z, ml_norm_g, w_out, x, gate1, norm_g, shift, scale, router_pad)


def _route_kernel(aff_ref, rank_ref, offs_ref, *, cap, iters):
    aff = aff_ref[...]
    n_exp = aff.shape[0]

    def count(mask):
        c = jnp.where(mask, 1.0, 0.0)
        return jnp.sum(jnp.sum(c, axis=1, keepdims=True), axis=2, keepdims=True)

    def bisect(_, carry):
        lo, hi = carry
        mid = 0.5 * (lo + hi)
        ok = count(aff >= mid) >= cap
        return jnp.where(ok, mid, lo), jnp.where(ok, hi, mid)

    _, hi = lax.fori_loop(0, iters, bisect,
                          (jnp.zeros((n_exp, 1, 1), F32), jnp.full((n_exp, 1, 1), 2.0, F32)))
    below = jnp.where(aff < hi, aff, -1.0)
    thr = jnp.max(jnp.max(below, axis=1, keepdims=True), axis=2, keepdims=True)
    gt = aff > thr
    eq = aff == thr
    need = cap - count(gt)

    rr = _iota((LANES, LANES), 0)
    cc = _iota((LANES, LANES), 1)
    upper = jnp.where(rr <= cc, 1.0, 0.0).astype(BF16)
    lower = jnp.where(cc < rr, 1.0, 0.0).astype(BF16)

    def prefix(x01):
        within = jnp.dot(x01.astype(BF16), upper, preferred_element_type=F32)
        offs = jnp.dot(lower, within.astype(BF16), preferred_element_type=F32)[:, LANES - 1:LANES]
        return within - x01 + offs, offs

    offs_all = jnp.zeros((ROUTE_BLOCKS, LANES), F32)
    for e in range(n_exp):
        gt_e = jnp.where(gt[e], 1.0, 0.0)
        eq_e = jnp.where(eq[e], 1.0, 0.0)
        eq_rank, _ = prefix(eq_e)
        sel = jnp.maximum(gt_e, jnp.where(eq_rank < need[e], eq_e, 0.0))
        rk, offs = prefix(sel)
        rk = jnp.where(sel > 0.0, rk, -1.0)
        rank_ref[e] = rk.astype(jnp.int32)
        taken = jnp.sum(jnp.sum(sel, axis=0, keepdims=True), axis=1, keepdims=True)
        offs_all = jnp.where(cc == e, offs, offs_all)
        offs_all = jnp.where(cc == n_exp + e, taken, offs_all)
    offs_ref[...] = offs_all.astype(jnp.int32)


def _route(aff_t, cap):
    n_exp, n = aff_t.shape
    assert n % LANES == 0 and n <= ROUTE_BLOCKS * LANES and cap <= n
    a = aff_t.reshape(n_exp, n // LANES, LANES)
    if n < ROUTE_BLOCKS * LANES:
        a = jnp.pad(a, ((0, 0), (0, ROUTE_BLOCKS - n // LANES), (0, 0)), constant_values=-1.0)
    blk = pl.BlockSpec((n_exp, ROUTE_BLOCKS, LANES), lambda i: (0, 0, 0))
    rank, offs = pl.pallas_call(
        functools.partial(_route_kernel, cap=cap, iters=ROUTE_ITERS),
        grid=(1,),
        in_specs=[blk],
        out_specs=[blk, pl.BlockSpec((ROUTE_BLOCKS, LANES), lambda i: (0, 0))],
        out_shape=[jax.ShapeDtypeStruct((n_exp, ROUTE_BLOCKS, LANES), jnp.int32),
                   jax.ShapeDtypeStruct((ROUTE_BLOCKS, LANES), jnp.int32)],
        compiler_params=_cparams(("arbitrary",)),
        name="route",
    )(a)
    offs_flat = jnp.concatenate([offs[:, :n_exp].T.reshape(-1), offs[0, n_exp:2 * n_exp]])
    return a, rank, offs_flat


def _gather_kernel(offs_ref, h_ref, rank_ref, gate_ref, xs_ref, gs_ref, idx_ref, *, cap, spb):
    e = pl.program_id(0)
    sb = pl.program_id(1)
    outs = (xs_ref, gs_ref, idx_ref)

    taken = offs_ref[pl.num_programs(0) * ROUTE_BLOCKS + e]

    @pl.when((sb == 0) & (taken == cap))
    def _():
        for ref in outs:
            ref[0, 0:GATHER_ALIGN, :] = jnp.zeros((GATHER_ALIGN, ref.shape[2]), ref.dtype)
            ref[0, cap:cap + GATHER_WIN, :] = jnp.zeros((GATHER_WIN, ref.shape[2]), ref.dtype)

    @pl.when((sb == 0) & (taken < cap))
    def _():
        for ref in outs:
            ref[...] = jnp.zeros(ref.shape, ref.dtype)

    lane = _iota((1, LANES), 1)

    def place(k, b, r0a, win):
        rows = _iota((win, LANES), 0)
        rrow = rank_ref[0, pl.ds(b, 1), :]
        grow = gate_ref[0, pl.ds(b, 1), :]
        trow = (b * LANES + lane).astype(F32)
        hit = (rrow - r0a) == rows
        hblk = h_ref[pl.ds(pl.multiple_of(k * LANES, LANES), LANES), :]
        onehot = jnp.where(hit, 1.0, 0.0).astype(BF16)
        rows_x = jnp.dot(onehot, hblk, preferred_element_type=F32).astype(BF16)
        rows_g = jnp.sum(jnp.where(hit, grow, 0.0), axis=1, keepdims=True)
        rows_i = jnp.sum(jnp.where(hit, trow, 0.0), axis=1, keepdims=True)
        head = pl.ds(r0a, GATHER_ALIGN)
        tail = pl.ds(r0a + GATHER_ALIGN, win - GATHER_ALIGN)
        for ref, val in zip(outs, (rows_x, rows_g, rows_i)):
            ref[0, head, :] += val[0:GATHER_ALIGN]
            ref[0, tail, :] = val[GATHER_ALIGN:win]

    group = min(spb, GATHER_GROUP)

    def body(i, carry):
        ks = [i * group + u for u in range(group)]
        bs = [sb * spb + k for k in ks]
        r0as, short = [], None
        for b in bs:
            r0 = offs_ref[e * ROUTE_BLOCKS + b]
            r0a = pl.multiple_of((r0 >> 4) << 4, GATHER_ALIGN)
            nxt = offs_ref[jnp.where(b == ROUTE_BLOCKS - 1, pl.num_programs(0) * ROUTE_BLOCKS + e,
                                     e * ROUTE_BLOCKS + b + 1)]
            fits = nxt - r0a <= GATHER_WIN_SHORT - GATHER_ALIGN
            short = fits if short is None else short & fits
            r0as.append(r0a)

        def run(win):
            for k, b, r0a in zip(ks, bs, r0as):
                place(k, b, r0a, win)

        lax.cond(short, lambda: run(GATHER_WIN_SHORT), lambda: run(GATHER_WIN))
        return carry

    lax.fori_loop(0, spb // group, body, 0)


def _gather(offs_flat, h2, rank, gate, cap):
    n, d = h2.shape
    n_exp = rank.shape[0]
    sbt = min(n, GATHER_TOKENS)
    spb = sbt // LANES
    cap_p = cap + GATHER_WIN
    blk = pl.BlockSpec((1, ROUTE_BLOCKS, LANES), lambda e, s, offs: (e, 0, 0))
    out = lambda w: pl.BlockSpec((1, cap_p, w), lambda e, s, offs: (e, 0, 0))
    return pl.pallas_call(
        functools.partial(_gather_kernel, cap=cap, spb=spb),
        grid_spec=pltpu.PrefetchScalarGridSpec(
            num_scalar_prefetch=1,
            grid=(n_exp, n // sbt),
            in_specs=[pl.BlockSpec((sbt, d), lambda e, s, offs: (s, 0)), blk, blk],
            out_specs=[out(d), out(1), out(1)]),
        out_shape=[jax.ShapeDtypeStruct((n_exp, cap_p, d), BF16),
                   jax.ShapeDtypeStruct((n_exp, cap_p, 1), F32),
                   jax.ShapeDtypeStruct((n_exp, cap_p, 1), F32)],
        compiler_params=_cparams(("arbitrary", "arbitrary")),
        name="moe_gather",
    )(offs_flat, h2, rank, gate)


def _ffn_tiles(caps, tm):
    tiles = [[(0, a, min(tm, caps[0] - a))] for a in range(0, caps[0], tm)]
    for s in range(1, len(caps)):
        if caps[s] <= FFN_RIDER_ROWS:
            tiles[-1].append((s, 0, caps[s]))
        else:
            tiles += [[(s, a, min(tm, caps[s] - a))] for a in range(0, caps[s], tm)]
    return tiles


def _ffn_kernel(*refs, caps, tm):
    ns = len(caps)
    xs_refs, (w1_ref, w3_ref, w2_ref) = refs[:ns], refs[ns:ns + 3]
    gs_refs, o_refs = refs[ns + 3:2 * ns + 3], refs[2 * ns + 3:]
    f = pl.program_id(1)

    @pl.when(f == 0)
    def _():
        for o_ref in o_refs:
            o_ref[...] = jnp.zeros(o_ref.shape, F32)

    w1 = w1_ref[0, 0].astype(BF16)
    w3 = w3_ref[0, 0].astype(BF16)
    w2 = w2_ref[0, 0].astype(BF16)
    for tile in _ffn_tiles(caps, tm):
        parts = [xs_refs[s][0, a:a + n, :] for s, a, n in tile]
        x = parts[0] if len(parts) == 1 else jnp.concatenate(parts, axis=0)
        a_ = jnp.dot(x, w1, preferred_element_type=F32)
        b_ = jnp.dot(x, w3, preferred_element_type=F32)
        hid = (a_ * _sigmoid(a_) * b_).astype(BF16)
        y = jnp.dot(hid, w2, preferred_element_type=F32)
        r = 0
        for s, a, n in tile:
            o_refs[s][0, a:a + n, :] += y[r:r + n]
            r += n

    @pl.when(f == pl.num_programs(1) - 1)
    def _():
        for o_ref, gs_ref in zip(o_refs, gs_refs):
            o_ref[0] = o_ref[0] * gs_ref[0]


def _ffn(row_sets, w1, w3, w2, layer):
    caps = tuple(cap for _, _, cap in row_sets)
    n_exp, _, d = row_sets[0][0].shape
    ff = w1.shape[3]
    tf = FFN_TF
    tm = min(caps[0], FFN_TM)
    rows = lambda cap, w: pl.BlockSpec((1, cap, w), lambda e, f: (e, 0, 0))
    outs = pl.pallas_call(
        functools.partial(_ffn_kernel, caps=caps, tm=tm),
        grid=(n_exp, ff // tf),
        in_specs=[rows(c, d) for c in caps]
        + [pl.BlockSpec((1, 1, d, tf), lambda e, f: (layer, e, 0, f)),
           pl.BlockSpec((1, 1, d, tf), lambda e, f: (layer, e, 0, f)),
           pl.BlockSpec((1, 1, tf, d), lambda e, f: (layer, e, f, 0))]
        + [rows(c, 1) for c in caps],
        out_specs=[rows(c, d) for c in caps],
        out_shape=[jax.ShapeDtypeStruct((n_exp, c, d), F32) for c in caps],
        compiler_params=_cparams(("arbitrary", "arbitrary")),
        name="moe_ffn",
    )(*[xs for xs, _, _ in row_sets], w1, w3, w2, *[gs for _, gs, _ in row_sets])
    return outs


def _combine_kernel(offs_ref, idx_ref, out_ref, y_ref, *, cap, tht):
    th = pl.program_id(0)
    e = pl.program_id(1)

    @pl.when(e == 0)
    def _():
        y_ref[...] = jnp.zeros(y_ref.shape, F32)

    bpt = tht // LANES
    first = e * ROUTE_BLOCKS + th * bpt
    start = offs_ref[first]
    is_last = th == pl.num_programs(0) - 1
    taken = offs_ref[pl.num_programs(1) * ROUTE_BLOCKS + e]
    end = jnp.where(is_last, taken, offs_ref[jnp.minimum(first + bpt, (e + 1) * ROUTE_BLOCKS - 1)])
    base = th * tht

    def token(r):
        return jnp.clip(idx_ref[e * cap + r] - base, 0, tht - 1)

    def single(r, carry):
        y_ref[pl.ds(token(r), 1), :] += out_ref[0, pl.ds(r, 1), :]
        return carry

    def tile(i, carry):
        r = pl.multiple_of(mid + SUBLANES * i, SUBLANES)
        rows = out_ref[0, pl.ds(r, SUBLANES), :]
        for g in range(0, SUBLANES, COMBINE_GROUP):
            ts = [token(r + g + u) for u in range(COMBINE_GROUP)]
            sums = [y_ref[pl.ds(ts[u], 1), :] + rows[g + u:g + u + 1, :] for u in range(COMBINE_GROUP)]
            for u in range(COMBINE_GROUP):
                y_ref[pl.ds(ts[u], 1), :] = sums[u]
        return carry

    mid = jnp.minimum(((start + SUBLANES - 1) >> 3) << 3, end)
    tail = jnp.maximum((end >> 3) << 3, mid)
    lax.fori_loop(start, mid, single, 0)
    lax.fori_loop(0, (tail - mid) >> 3, tile, 0)
    lax.fori_loop(tail, end, single, 0)


def _combine(offs_flat, idx_flat, out, n):
    n_exp, cap, d = out.shape
    tht = min(n, COMBINE_TOKENS)
    return pl.pallas_call(
        functools.partial(_combine_kernel, cap=cap, tht=tht),
        grid_spec=pltpu.PrefetchScalarGridSpec(
            num_scalar_prefetch=2,
            grid=(n // tht, n_exp),
            in_specs=[pl.BlockSpec((1, cap, d), lambda t, e, offs, idx: (e, 0, 0))],
            out_specs=pl.BlockSpec((tht, d), lambda t, e, offs, idx: (t, 0))),
        out_shape=jax.ShapeDtypeStruct((n, d), F32),
        compiler_params=_cparams(("arbitrary", "arbitrary")),
        name="moe_combine",
    )(offs_flat, idx_flat, out)


def _moe(routed_sets, w1, w3, w2, layer):
    plans = []
    for h2, aff_t in routed_sets:
        n = h2.shape[0]
        cap = CAPACITY_FACTOR * n // aff_t.shape[0]
        gate, rank, offs_flat = _route(aff_t, cap)
        xs, gs, idx = _gather(offs_flat, h2, rank, gate, cap)
        plans.append((offs_flat, idx[:, :cap, 0].astype(jnp.int32).reshape(-1), xs, gs, cap, n))
    outs = _ffn([(p[2], p[3], p[4]) for p in plans], w1, w3, w2, layer)
    return [_combine(p[0], p[1], out, p[5]) for p, out in zip(plans, outs)]


def _final_kernel(x_ref, y_ref, g2_ref, ng_ref, o_ref):
    x = x_ref[...] + g2_ref[...] * y_ref[...]
    ms = jnp.mean(x * x, axis=-1, keepdims=True)
    o_ref[...] = x * lax.rsqrt(ms + EPS) * ng_ref[...]


def _final(x, y, g2, norm_g):
    n, d = x.shape
    tn = min(n, ROW_TILE)
    row = pl.BlockSpec((tn, d), lambda i: (i, 0))
    vec = pl.BlockSpec((1, d), lambda i: (0, 0))
    return pl.pallas_call(
        _final_kernel,
        grid=(n // tn,),
        in_specs=[row, row, vec, vec],
        out_specs=row,
        out_shape=jax.ShapeDtypeStruct((n, d), F32),
        compiler_params=_cparams(("arbitrary",)),
        name="final_norm",
    )(x, y, g2, norm_g)


def _rope_tables(n):
    pos = jnp.arange(n)
    row, colp = pos // GRID_W, pos % GRID_W
    axis_dim = HEAD_DIM // 2
    inv_freq = ROPE_THETA ** (-jnp.arange(0, axis_dim, 2, dtype=F32) / axis_dim)

    def axis_angles(p):
        a = p.astype(F32)[:, None] * inv_freq[None, :]
        return jnp.concatenate([a, a], axis=-1)

    ang = jnp.concatenate([axis_angles(row), axis_angles(colp)], axis=-1)
    ang = jnp.concatenate([ang, ang], axis=-1)
    return jnp.cos(ang), jnp.sin(ang)


def _zero_state():
    return (jnp.zeros((GROUP_W, GROUP_W), F32), jnp.zeros((SUBLANES, GROUP_W), F32),
            jnp.zeros((SUBLANES, LANES), F32))


def kernel(x, c, ctx, c_ctx, ada_w, ada_b, norm1_g, w_in, conv_w, cmlp_norm_g, cmlp_ws, cmlp_bs, q_norm_g, k_norm_g, ml_igate_b, ml_fgate_b, ml_norm_g, w_out, norm2_g, router_w, exp_w1, exp_w3, exp_w2, final_norm_g):
    assert x.shape[0] == 1 and ctx.shape[0] == 1
    depth, d, d_in = w_in.shape
    n_exp = router_w.shape[-1]
    assert d_in <= N_IN_BLOCKS * LANES and d == 4 * GROUP_W
    xs = x[0]
    xc = ctx[0]
    n = xs.shape[0]

    mod = _modulation(jnp.stack([c[0], c_ctx], axis=1), ada_w, ada_b)
    tables = _rope_tables(n)
    row = lambda v: v.reshape(1, -1)

    res_x = None
    res_c = None
    for l in range(depth):
        parts = [[mod[l, s:s + 1, j * d:(j + 1) * d] for j in range(6)] for s in range(2)]
        sh1, sc1, g1, sh2, sc2, g2 = parts[0]
        csh1, csc1, cg1, csh2, csc2, cg2 = parts[1]
        w_in_l = jnp.pad(w_in[l], ((0, 0), (0, N_IN_BLOCKS * LANES - d_in))).astype(BF16)
        w_out_l = w_out[l].astype(BF16)
        cmlp_ws_l = cmlp_ws[l].astype(BF16)
        cmlp_bias = jnp.repeat(cmlp_bs[l].T, HEAD_DIM, axis=1)
        qg = jnp.tile(q_norm_g[l], 2).reshape(1, LANES)
        kg = jnp.tile(k_norm_g[l], 2).reshape(1, LANES)
        gate_b = jnp.pad(jnp.concatenate([ml_igate_b[l], ml_fgate_b[l]]), (0, LANES - 16)).reshape(1, LANES)
        router_pad = jnp.pad(router_w[l], ((0, 0), (0, LANES - n_exp)))
        update_ctx = l < depth - 1

        zc, xc = _inproj(xc, res_c, row(norm1_g[l]), csh1, csc1, w_in_l)
        n_ctx = xc.shape[0]
        qc_, ktc, vdc = _attprep(zc, qg, kg, None)
        hcf, hcb, st_f, st_b = _mlstm(zc, gate_b, _zero_state(), _zero_state())
        if update_ctx:
            conv_c, cmlp_c = _local_mixers(zc, conv_w[l], row(cmlp_norm_g[l]), cmlp_ws_l, cmlp_bias)
            att_c = _flash(qc_, ktc, vdc, n_ctx)
            xc, hc2, affc = _outproj((conv_c, cmlp_c, att_c, hcf, hcb), zc, row(ml_norm_g[l]), w_out_l, xc, cg1,
                                     row(norm2_g[l]), csh2, csc2, router_pad, n_exp)

        z, xs = _inproj(xs, res_x, row(norm1_g[l]), sh1, sc1, w_in_l)
        conv_x, cmlp_x = _local_mixers(z, conv_w[l], row(cmlp_norm_g[l]), cmlp_ws_l, cmlp_bias)
        q_, kt, vd = _attprep(z, qg, kg, tables)
        att_x = _flash(q_, jnp.concatenate([ktc, kt], axis=1), jnp.concatenate([vdc, vd], axis=0), n_ctx + n)
        hf, hb, _, _ = _mlstm(z, gate_b, st_f, st_b)
        xs, h2, aff = _outproj((conv_x, cmlp_x, att_x, hf, hb), z, row(ml_norm_g[l]), w_out_l, xs, g1,
                               row(norm2_g[l]), sh2, sc2, router_pad, n_exp)
        routed = [(h2, aff)] + ([(hc2, affc)] if update_ctx else [])
        ys = _moe(routed, exp_w1, exp_w3, exp_w2, l)
        res_x = (ys[0], g2)
        if update_ctx:
            res_c = (ys[1], cg2)

    return _final(xs, res_x[0], res_x[1], row(final_norm_g))[None]
```

```python
import functools

import jax
import jax.numpy as jnp
from jax import lax
from jax.experimental import pallas as pl
from jax.experimental.pallas import tpu as pltpu

F32 = jnp.float32
BF16 = jnp.bfloat16
HIGHEST = lax.Precision.HIGHEST

EPS = 1e-6
LOG2E = 1.4426950408889634
GRID_W = 64
ROPE_THETA = 10000.0
CAPACITY_FACTOR = 2
GROUP_W = 256
HEAD_DIM = 64
CHUNK = 128
N_IN_BLOCKS = 23

LANES = 128
SUBLANES = 8
VMEM_LIMIT = 56 * 1024 * 1024

ROUTE_BLOCKS = 128
ROUTE_ITERS = 48
GATHER_ALIGN = 16
GATHER_WIN = 160
GATHER_WIN_SHORT = 64
GATHER_GROUP = 8
COMBINE_GROUP = 4
FFN_RIDER_ROWS = 64

MOD_TILE = 1536
ROW_TILE = 512
MIXER_TILE = 256
GATHER_TOKENS = 2048
FFN_TF = 512
FFN_TM = 512
COMBINE_TOKENS = 4096
FLASH_TQ = 512
FLASH_TK_MAX = 3328

NT_DIMS = (((1,), (1,)), ((), ()))


def _cparams(sem, vmem=None):
    return pltpu.CompilerParams(dimension_semantics=sem, vmem_limit_bytes=vmem or VMEM_LIMIT)


def _sigmoid(x):
    return 1.0 / (1.0 + jnp.exp(-x))


def _iota(shape, dim):
    return lax.broadcasted_iota(jnp.int32, shape, dim)


def _split(x, pieces):
    out = []
    for _ in range(pieces - 1):
        p = x.astype(BF16)
        out.append(p)
        x = x - p.astype(F32)
    return out + [x.astype(BF16)]


def _dot_f32_lhs(x, w):
    w = w.astype(BF16)
    return sum(jnp.dot(p, w, preferred_element_type=F32) for p in _split(x, 3))


def _dot_f32_rhs(w, x):
    w = w.astype(BF16)
    return sum(jnp.dot(w, p, preferred_element_type=F32) for p in _split(x, 3))


def _mod_kernel(cc_ref, w_ref, b_ref, o_ref):
    cc = cc_ref[...]
    s = cc * _sigmoid(cc)
    w = w_ref[0]
    b = b_ref[0]
    o_ref[0, 0:1, :] = jnp.sum(w * s[:, 0:1], axis=0, keepdims=True) + b
    o_ref[0, 1:2, :] = jnp.sum(w * s[:, 1:2], axis=0, keepdims=True) + b


def _modulation(cc, ada_w, ada_b):
    depth, d, m = ada_w.shape
    tn = MOD_TILE
    return pl.pallas_call(
        _mod_kernel,
        grid=(depth, m // tn),
        in_specs=[pl.BlockSpec((d, 2), lambda l, j: (0, 0)),
                  pl.BlockSpec((1, d, tn), lambda l, j: (l, 0, j)),
                  pl.BlockSpec((1, 1, tn), lambda l, j: (l, 0, j))],
        out_specs=pl.BlockSpec((1, 2, tn), lambda l, j: (l, 0, j)),
        out_shape=jax.ShapeDtypeStruct((depth, 2, m), F32),
        compiler_params=_cparams(("arbitrary", "arbitrary")),
        name="modulation",
    )(cc, ada_w, ada_b.reshape(depth, 1, m))


def _inproj_kernel(*refs, has_res):
    if has_res:
        x_ref, y_ref, g2_ref, ng_ref, sh_ref, sc_ref, w_ref, z_ref, xr_ref = refs
        x = x_ref[...] + g2_ref[...] * y_ref[...]
        xr_ref[...] = x
    else:
        x_ref, ng_ref, sh_ref, sc_ref, w_ref, z_ref = refs
        x = x_ref[...]
    ms = jnp.mean(x * x, axis=-1, keepdims=True)
    h = x * lax.rsqrt(ms + EPS) * ng_ref[...]
    h = h * (1.0 + sc_ref[...]) + sh_ref[...]
    z_ref[...] = jnp.dot(h.astype(BF16), w_ref[...], preferred_element_type=F32)


def _inproj(x, res, norm_g, shift, scale, w_in):
    n, d = x.shape
    dz = w_in.shape[1]
    tn = min(n, ROW_TILE)
    row = pl.BlockSpec((tn, d), lambda i: (i, 0))
    vec = pl.BlockSpec((1, d), lambda i: (0, 0))
    wsp = pl.BlockSpec((d, dz), lambda i: (0, 0))
    zsp = pl.BlockSpec((tn, dz), lambda i: (i, 0))
    if res is None:
        z = pl.pallas_call(
            functools.partial(_inproj_kernel, has_res=False),
            grid=(n // tn,),
            in_specs=[row, vec, vec, vec, wsp],
            out_specs=zsp,
            out_shape=jax.ShapeDtypeStruct((n, dz), F32),
            compiler_params=_cparams(("arbitrary",)),
            name="inproj",
        )(x, norm_g, shift, scale, w_in)
        return z, x
    y, g2 = res
    z, xr = pl.pallas_call(
        functools.partial(_inproj_kernel, has_res=True),
        grid=(n // tn,),
        in_specs=[row, row, vec, vec, vec, vec, wsp],
        out_specs=[zsp, row],
        out_shape=[jax.ShapeDtypeStruct((n, dz), F32), jax.ShapeDtypeStruct((n, d), F32)],
        compiler_params=_cparams(("arbitrary",)),
        name="inproj_res",
    )(x, y, g2, norm_g, shift, scale, w_in)
    return z, xr


def _local_kernel(cx_ref, cb_ref, cc_ref, u_ref, v_ref, px_ref, pc_ref, nx_ref, nc_ref,
                  cw_ref, g_ref, ws_ref, bs_ref, conv_ref, cmlp_ref, *, tn):
    i = pl.program_id(0)
    last = pl.num_programs(0) - 1
    cx = cc_ref[...] * cx_ref[...]
    prev = (pc_ref[...] * px_ref[...])[SUBLANES - 1:SUBLANES, :]
    nxt = (nc_ref[...] * nx_ref[...])[0:1, :]
    prev = jnp.where(i == 0, 0.0, prev)
    nxt = jnp.where(i == last, 0.0, nxt)
    row = _iota((tn, 1), 0)
    up = jnp.where(row == 0, prev, pltpu.roll(cx, 1, 0))
    dn = jnp.where(row == tn - 1, nxt, pltpu.roll(cx, tn - 1, 0))
    w = cw_ref[...]
    conv_ref[...] = cb_ref[...] * (w[0:1] * up + w[1:2] * cx + w[2:3] * dn)

    v = v_ref[...]
    vn = v * lax.rsqrt(jnp.mean(v * v, axis=-1, keepdims=True) + EPS) * g_ref[...]
    head = _iota((1, GROUP_W), 1) >> 6
    for c in range(tn // CHUNK):
        sl = slice(c * CHUNK, (c + 1) * CHUNK)
        vc = vn[sl]
        acc = bs_ref[...]
        for h in range(GROUP_W // HEAD_DIM):
            vm = jnp.where(head == h, vc, 0.0).astype(BF16)
            acc = acc + jnp.dot(ws_ref[h], vm, preferred_element_type=F32)
        cmlp_ref[sl, :] = u_ref[sl, :] * acc


def _local_mixers(z, conv_w, cmlp_g, cmlp_ws, cmlp_bias):
    n = z.shape[0]
    tn = min(n, MIXER_TILE)
    tb = tn // SUBLANES
    nb8 = n // SUBLANES

    def col(j):
        return pl.BlockSpec((tn, GROUP_W), lambda i: (i, j))

    def prev(j):
        return pl.BlockSpec((SUBLANES, GROUP_W), lambda i: (jnp.maximum(i * tb - 1, 0), j))

    def nxt(j):
        return pl.BlockSpec((SUBLANES, GROUP_W), lambda i: (jnp.minimum((i + 1) * tb, nb8 - 1), j))

    out = pl.BlockSpec((tn, GROUP_W), lambda i: (i, 0))
    return pl.pallas_call(
        functools.partial(_local_kernel, tn=tn),
        grid=(n // tn,),
        in_specs=[col(0), col(1), col(2), col(3), col(4), prev(0), prev(2), nxt(0), nxt(2),
                  pl.BlockSpec((3, GROUP_W), lambda i: (0, 0)),
                  pl.BlockSpec((1, GROUP_W), lambda i: (0, 0)),
                  pl.BlockSpec((4, CHUNK, CHUNK), lambda i: (0, 0, 0)),
                  pl.BlockSpec((CHUNK, GROUP_W), lambda i: (0, 0))],
        out_specs=[out, out],
        out_shape=[jax.ShapeDtypeStruct((n, GROUP_W), F32)] * 2,
        compiler_params=_cparams(("arbitrary",)),
        name="local_mixers",
    )(z, z, z, z, z, z, z, z, z, conv_w, cmlp_g, cmlp_ws, cmlp_bias)


def _attprep_kernel(*refs, rope):
    q_ref, kv_ref, qg_ref, kg_ref = refs[:4]
    if rope:
        cos_ref, sin_ref = refs[4:6]
    qo_ref, kt_ref, vd_ref = refs[-3:]
    q = q_ref[...]
    kv = kv_ref[...]
    k = kv[:, :LANES]
    v = kv[:, LANES:]
    same_head = (_iota((LANES, LANES), 0) >> 6) == (_iota((LANES, LANES), 1) >> 6)
    bd = jnp.where(same_head, 1.0 / HEAD_DIM, 0.0)
    lane = _iota((1, LANES), 1)
    first_half = (lane & 31) < 16

    def head_norm(x, g):
        ms = _dot_f32_lhs(x * x, bd)
        return x * lax.rsqrt(ms + EPS) * g

    def rotary(x):
        if not rope:
            return x
        rot = jnp.where(first_half, -pltpu.roll(x, LANES - 16, 1), pltpu.roll(x, 16, 1))
        return x * cos_ref[...] + rot * sin_ref[...]

    for half in range(2):
        sl = slice(half * LANES, (half + 1) * LANES)
        qh = rotary(head_norm(q[:, sl], qg_ref[...]))
        qo_ref[:, sl] = (qh * (HEAD_DIM ** -0.5 * LOG2E)).astype(BF16)
    kn = rotary(head_norm(k, kg_ref[...]))
    kt = kn.T.astype(BF16)
    kt_ref[0:64, :] = kt[0:64]
    kt_ref[64:128, :] = kt[0:64]
    kt_ref[128:192, :] = kt[64:128]
    kt_ref[192:256, :] = kt[64:128]
    vr = pltpu.roll(v, HEAD_DIM, 1)
    low = lane < HEAD_DIM
    vd_ref[:, 0:LANES] = jnp.where(low, v, vr).astype(BF16)
    vd_ref[:, LANES:] = jnp.where(low, vr, v).astype(BF16)


def _attprep(z, q_g, k_g, tables):
    n = z.shape[0]
    tn = min(n, MIXER_TILE)
    rope = tables is not None
    in_specs = [pl.BlockSpec((tn, GROUP_W), lambda i: (i, 5)),
                pl.BlockSpec((tn, GROUP_W), lambda i: (i, 6)),
                pl.BlockSpec((1, LANES), lambda i: (0, 0)),
                pl.BlockSpec((1, LANES), lambda i: (0, 0))]
    args = [z, z, q_g, k_g]
    if rope:
        in_specs += [pl.BlockSpec((tn, LANES), lambda i: (i, 0))] * 2
        args += list(tables)
    return pl.pallas_call(
        functools.partial(_attprep_kernel, rope=rope),
        grid=(n // tn,),
        in_specs=in_specs,
        out_specs=[pl.BlockSpec((tn, GROUP_W), lambda i: (i, 0)),
                   pl.BlockSpec((GROUP_W, tn), lambda i: (0, i)),
                   pl.BlockSpec((tn, GROUP_W), lambda i: (i, 0))],
        out_shape=[jax.ShapeDtypeStruct((n, GROUP_W), BF16),
                   jax.ShapeDtypeStruct((GROUP_W, n), BF16),
                   jax.ShapeDtypeStruct((n, GROUP_W), BF16)],
        compiler_params=_cparams(("arbitrary",)),
        name="attprep_rope" if rope else "attprep",
    )(*args)


def _flash_kernel(q_ref, k_ref, v_ref, o_ref, s_ref, mx_ref, *, tk, nk):
    q = q_ref[...]
    tq = q.shape[0]
    low = _iota((1, LANES), 1) < HEAD_DIM
    zero = jnp.zeros_like(q)
    qa = jnp.where(low, q, zero)
    qb = jnp.where(low, zero, q)

    def scores(j, slot):
        k = k_ref[:, pl.ds(pl.multiple_of(j * tk, LANES), tk)]
        for h, qh in enumerate((qa, qb)):
            s = jnp.dot(qh, k, preferred_element_type=F32)
            s_ref[slot, h] = s
            mx_ref[slot, h] = jnp.max(s, axis=1, keepdims=True)

    def consume(j, slot, carry):
        m_a, acc_a, m_b, acc_b = carry
        v = v_ref[pl.ds(pl.multiple_of(j * tk, LANES), tk), :]
        ones = jnp.ones_like(v)
        n_a = jnp.maximum(m_a, mx_ref[slot, 0])
        n_b = jnp.maximum(m_b, mx_ref[slot, 1])
        p_a = jnp.exp2(s_ref[slot, 0] - n_a).astype(BF16)
        p_b = jnp.exp2(s_ref[slot, 1] - n_b).astype(BF16)
        acc_a = acc_a * jnp.exp2(m_a - n_a) + jnp.dot(p_a, jnp.where(low, v, ones), preferred_element_type=F32)
        acc_b = acc_b * jnp.exp2(m_b - n_b) + jnp.dot(p_b, jnp.where(low, ones, v), preferred_element_type=F32)
        return n_a, acc_a, n_b, acc_b

    neg = jnp.full((tq, 1), -jnp.inf, F32)
    za = jnp.zeros((tq, LANES), F32)
    carry = (neg, za, neg, za)
    scores(0, 0)
    if nk % 2 == 0 and nk >= 4:
        def pair(i, carry):
            j = 2 * i
            scores(j + 1, 1)
            carry = consume(j, 0, carry)
            scores(j + 2, 0)
            return consume(j + 1, 1, carry)

        carry = lax.fori_loop(0, nk // 2 - 1, pair, carry)
        scores(nk - 1, 1)
        carry = consume(nk - 2, 0, carry)
        carry = consume(nk - 1, 1, carry)
    else:
        for j in range(nk):
            if j + 1 < nk:
                scores(j + 1, (j + 1) % 2)
            carry = consume(j, j % 2, carry)
    _, acc_a, _, acc_b = carry
    o_ref[...] = jnp.where(low, acc_a / pltpu.roll(acc_a, HEAD_DIM, 1), acc_b / pltpu.roll(acc_b, HEAD_DIM, 1))


def _kv_tile(nk):
    best = LANES
    for t in range(LANES, FLASH_TK_MAX + 1, LANES):
        if nk % t == 0:
            best = t
    return best


def _flash(q, kt, vd, nk):
    n = q.shape[0]
    tq = min(n, FLASH_TQ)
    tk = _kv_tile(nk)
    return pl.pallas_call(
        functools.partial(_flash_kernel, tk=tk, nk=nk // tk),
        grid=(2, n // tq),
        in_specs=[pl.BlockSpec((tq, LANES), lambda g, i: (i, g)),
                  pl.BlockSpec((LANES, nk), lambda g, i: (g, 0)),
                  pl.BlockSpec((nk, LANES), lambda g, i: (0, g))],
        out_specs=pl.BlockSpec((tq, LANES), lambda g, i: (i, g)),
        out_shape=jax.ShapeDtypeStruct((n, GROUP_W), F32),
        scratch_shapes=[pltpu.VMEM((2, 2, tq, tk), F32), pltpu.VMEM((2, 2, tq, 1), F32)],
        compiler_params=_cparams(("arbitrary", "arbitrary")),
        name="flash_attention",
    )(q, kt, vd)


def _mlstm_chunk(q, k, v, g, h_ref, c_ref, n_ref, m_ref, *, fwd):
    off = 0 if fwd else 4
    nh = GROUP_W // HEAD_DIM
    ks = k * (HEAD_DIM ** -0.5)
    lane = _iota((1, LANES), 1)
    lsig = jnp.minimum(g, 0.0) - jnp.log(1.0 + jnp.exp(-jnp.abs(g)))
    gcol = jnp.where((lane >= 8) & (lane < 16), lsig, g)
    lfcol = pltpu.roll(gcol, LANES - 8, 1)
    rr = _iota((CHUNK, CHUNK), 0)
    cc = _iota((CHUNK, CHUNK), 1)
    sees = (cc <= rr) if fwd else (cc >= rr)
    tri = jnp.where(sees, 1.0, 0.0)
    bc = _dot_f32_rhs(tri, lfcol)
    bct = bc.T
    gt = gcol.T
    total = bc[CHUNK - 1:CHUNK, :] if fwd else bc[0:1, :]
    cp = c_ref[...]
    npv = n_ref[0:1, :]
    mp = m_ref[0:1, :]
    head = _iota((1, GROUP_W), 1) >> 6
    qb = q.astype(BF16)
    kb = ks.astype(BF16)
    qc = lax.dot_general(qb, cp.astype(BF16), NT_DIMS, preferred_element_type=F32)
    heads = range(nh)
    stack = lambda parts: jnp.concatenate(parts, axis=0)
    restack = lambda x: stack([x[:, h * CHUNK:(h + 1) * CHUNK] for h in heads])
    pick = jnp.where(_iota((LANES, nh * CHUNK), 0) == off + (_iota((LANES, nh * CHUNK), 1) >> 7), 1.0, 0.0)
    eh = jnp.where((_iota((GROUP_W, nh * CHUNK), 0) >> 6) == (_iota((GROUP_W, nh * CHUNK), 1) >> 7), 1.0, 0.0)
    nq = restack(jnp.dot((q * npv).astype(BF16), eh.astype(BF16), preferred_element_type=F32))
    spread = _dot_f32_lhs(stack([bc, bc + mp]), pick)
    bc_t = restack(spread[0:CHUNK])
    inter = restack(spread[CHUNK:2 * CHUNK])
    row = stack([jnp.broadcast_to(gt[off + h:off + h + 1, :] - bct[off + h:off + h + 1, :],
                                  (CHUNK, CHUNK)) for h in heads])
    dm = jnp.where(stack([sees] * nh), bc_t + row, -jnp.inf)
    m_t = jnp.maximum(inter, jnp.max(dm, axis=1, keepdims=True))
    q_stack = stack([jnp.where(head == h, q, 0.0) for h in heads]).astype(BF16)
    v_stack = stack([jnp.where(head == h, v, 0.0) for h in heads]).astype(BF16)
    s = lax.dot_general(q_stack, kb, NT_DIMS, preferred_element_type=F32) * jnp.exp(dm - m_t)
    wi = jnp.exp(inter - m_t)
    s_lanes = jnp.concatenate([s[h * CHUNK:(h + 1) * CHUNK] for h in heads], axis=1).astype(BF16)
    num = jnp.dot(s_lanes, v_stack, preferred_element_type=F32)
    den = jnp.sum(s, axis=1, keepdims=True) + wi * nq
    dd = jnp.maximum(jnp.abs(den), jnp.exp(-m_t))
    wi_x = jnp.zeros((CHUNK, GROUP_W), F32)
    den_x = jnp.ones((CHUNK, GROUP_W), F32)
    for h in heads:
        rows = slice(h * CHUNK, (h + 1) * CHUNK)
        wi_x = jnp.where(head == h, jnp.concatenate([wi[rows]] * 2, axis=1), wi_x)
        den_x = jnp.where(head == h, jnp.concatenate([dd[rows]] * 2, axis=1), den_x)
    h_ref[...] = (num + wi_x * qc) / den_x

    a = total - bc + gcol
    m_loc = jnp.max(a, axis=0, keepdims=True)
    w = jnp.exp(a - m_loc)
    m_new = jnp.maximum(total + mp, m_loc)
    d_old = jnp.exp(total + mp - m_new)
    d_loc = jnp.exp(m_loc - m_new)
    pick_w = jnp.where(_iota((LANES, GROUP_W), 0) == off + (_iota((LANES, GROUP_W), 1) >> 6), 1.0, 0.0)
    w_x = _dot_f32_lhs(w, pick_w)
    decay = _dot_f32_lhs(stack([d_old, d_loc, jnp.zeros((SUBLANES - 2, LANES), F32)]), pick_w)
    dold_x = decay[0:1]
    dloc_x = decay[1:2]
    c_loc = jnp.dot((w_x * v).T.astype(BF16), kb, preferred_element_type=F32)
    same_head = (_iota((GROUP_W, GROUP_W), 0) >> 6) == (_iota((GROUP_W, GROUP_W), 1) >> 6)
    c_ref[...] = cp * dold_x + jnp.where(same_head, c_loc, 0.0) * dloc_x
    n_new = npv * dold_x + jnp.sum(w_x * ks, axis=0, keepdims=True) * dloc_x
    n_ref[...] = jnp.broadcast_to(n_new, n_ref.shape)
    m_ref[...] = jnp.broadcast_to(m_new, m_ref.shape)


def _mlstm_kernel(qf_ref, kf_ref, vf_ref, gf_ref, qb_ref, kb_ref, vb_ref, gb_ref, bias_ref,
                  c0f_ref, n0f_ref, m0f_ref, c0b_ref, n0b_ref, m0b_ref,
                  hf_ref, hb_ref, cf_ref, nf_ref, mf_ref, cb_ref, nb_ref, mb_ref):
    @pl.when(pl.program_id(0) == 0)
    def _():
        for dst, src in ((cf_ref, c0f_ref), (nf_ref, n0f_ref), (mf_ref, m0f_ref),
                         (cb_ref, c0b_ref), (nb_ref, n0b_ref), (mb_ref, m0b_ref)):
            dst[...] = src[...]

    _mlstm_chunk(qf_ref[...], kf_ref[...], vf_ref[...], gf_ref[...] + bias_ref[...],
                 hf_ref, cf_ref, nf_ref, mf_ref, fwd=True)
    _mlstm_chunk(qb_ref[...], kb_ref[...], vb_ref[...], gb_ref[...] + bias_ref[...],
                 hb_ref, cb_ref, nb_ref, mb_ref, fwd=False)


def _mlstm(z, gate_b, state_f, state_b):
    n = z.shape[0]
    nc = n // CHUNK

    def cols(order):
        return [pl.BlockSpec((CHUNK, GROUP_W), lambda c, j=j: (order(c), j)) for j in (7, 8, 9)] + [
            pl.BlockSpec((CHUNK, LANES), lambda c: (order(c), N_IN_BLOCKS - 1))]

    def full(shape):
        return pl.BlockSpec(shape, lambda c: (0,) * len(shape))

    fo = lambda c: c
    bo = lambda c: nc - 1 - c
    st_shapes = [(GROUP_W, GROUP_W), (SUBLANES, GROUP_W), (SUBLANES, LANES)]
    st_specs = [full(s) for s in st_shapes]
    outs = pl.pallas_call(
        _mlstm_kernel,
        grid=(nc,),
        in_specs=cols(fo) + cols(bo) + [full((1, LANES))] + st_specs * 2,
        out_specs=[pl.BlockSpec((CHUNK, GROUP_W), lambda c: (fo(c), 0)),
                   pl.BlockSpec((CHUNK, GROUP_W), lambda c: (bo(c), 0))] + st_specs * 2,
        out_shape=[jax.ShapeDtypeStruct((n, GROUP_W), F32)] * 2
        + [jax.ShapeDtypeStruct(s, F32) for s in st_shapes] * 2,
        compiler_params=_cparams(("arbitrary",)),
        name="mlstm",
    )(z, z, z, z, z, z, z, z, gate_b, *state_f, *state_b)
    return outs[0], outs[1], tuple(outs[2:5]), tuple(outs[5:8])


def _outproj_kernel(cv_ref, cm_ref, at_ref, hf_ref, hb_ref, po_ref, mg_ref, w_ref, x_ref, g1_ref, ng_ref,
                    sh_ref, sc_ref, rw_ref, xo_ref, h_ref, aff_ref, *, n_exp):
    hs = hf_ref[...] + hb_ref[...]
    same_head = (_iota((GROUP_W, GROUP_W), 0) >> 6) == (_iota((GROUP_W, GROUP_W), 1) >> 6)
    hms = _dot_f32_lhs(hs * hs, jnp.where(same_head, 1.0 / HEAD_DIM, 0.0))
    ml = _sigmoid(po_ref[...]) * (hs * lax.rsqrt(hms + EPS) * mg_ref[...])
    mix = jnp.zeros(x_ref.shape, F32)
    for j, val in enumerate((cv_ref[...], cm_ref[...], at_ref[...], ml)):
        mix = mix + jnp.dot(val.astype(BF16), w_ref[j * GROUP_W:(j + 1) * GROUP_W, :],
                            preferred_element_type=F32)
    x = x_ref[...] + g1_ref[...] * mix
    xo_ref[...] = x
    ms = jnp.mean(x * x, axis=-1, keepdims=True)
    h = x * lax.rsqrt(ms + EPS) * ng_ref[...]
    h = h * (1.0 + sc_ref[...]) + sh_ref[...]
    h_ref[...] = h.astype(BF16)
    h_hi, h_lo = _split(h, 2)
    w_hi, w_lo = _split(rw_ref[...], 2)
    logits = (jnp.dot(h_hi, w_hi, preferred_element_type=F32) + jnp.dot(h_lo, w_hi, preferred_element_type=F32)
              + jnp.dot(h_hi, w_lo, preferred_element_type=F32))
    lt = logits.T[0:n_exp, :]
    e = jnp.exp(lt - jnp.max(lt, axis=0, keepdims=True))
    aff_ref[...] = e / jnp.sum(e, axis=0, keepdims=True)


def _outproj(mixers, z, ml_norm_g, w_out, x, gate1, norm_g, shift, scale, router_pad, n_exp):
    n, d = x.shape
    tn = min(n, ROW_TILE)
    mix_spec = pl.BlockSpec((tn, GROUP_W), lambda i: (i, 0))
    row = pl.BlockSpec((tn, d), lambda i: (i, 0))
    vec = pl.BlockSpec((1, d), lambda i: (0, 0))
    return pl.pallas_call(
        functools.partial(_outproj_kernel, n_exp=n_exp),
        grid=(n // tn,),
        in_specs=[mix_spec] * 5 + [pl.BlockSpec((tn, GROUP_W), lambda i: (i, 10)),
                                   pl.BlockSpec((1, GROUP_W), lambda i: (0, 0)),
                                   pl.BlockSpec((d, d), lambda i: (0, 0)), row, vec, vec, vec, vec,
                                   pl.BlockSpec((d, LANES), lambda i: (0, 0))],
        out_specs=[row, row, pl.BlockSpec((n_exp, tn), lambda i: (0, i))],
        out_shape=[jax.ShapeDtypeStruct((n, d), F32), jax.ShapeDtypeStruct((n, d), BF16),
                   jax.ShapeDtypeStruct((n_exp, n), F32)],
        compiler_params=_cparams(("arbitrary",)),
        name="outproj_router",
    )(*mixers, z, ml_norm_g, w_out, x, gate1, norm_g, shift, scale, router_pad)


def _route_kernel(aff_ref, rank_ref, offs_ref, *, cap, iters):
    aff = aff_ref[...]
    n_exp = aff.shape[0]

    def count(mask):
        c = jnp.where(mask, 1.0, 0.0)
        return jnp.sum(jnp.sum(c, axis=1, keepdims=True), axis=2, keepdims=True)

    def bisect(_, carry):
        lo, hi = carry
        mid = 0.5 * (lo + hi)
        ok = count(aff >= mid) >= cap
        return jnp.where(ok, mid, lo), jnp.where(ok, hi, mid)

    _, hi = lax.fori_loop(0, iters, bisect,
                          (jnp.zeros((n_exp, 1, 1), F32), jnp.full((n_exp, 1, 1), 2.0, F32)))
    below = jnp.where(aff < hi, aff, -1.0)
    thr = jnp.max(jnp.max(below, axis=1, keepdims=True), axis=2, keepdims=True)
    gt = aff > thr
    eq = aff == thr
    need = cap - count(gt)

    rr = _iota((LANES, LANES), 0)
    cc = _iota((LANES, LANES), 1)
    upper = jnp.where(rr <= cc, 1.0, 0.0).astype(BF16)
    lower = jnp.where(cc < rr, 1.0, 0.0).astype(BF16)

    def prefix(x01):
        within = jnp.dot(x01.astype(BF16), upper, preferred_element_type=F32)
        offs = jnp.dot(lower, within.astype(BF16), preferred_element_type=F32)[:, LANES - 1:LANES]
        return within - x01 + offs, offs

    offs_all = jnp.zeros((ROUTE_BLOCKS, LANES), F32)
    for e in range(n_exp):
        gt_e = jnp.where(gt[e], 1.0, 0.0)
        eq_e = jnp.where(eq[e], 1.0, 0.0)
        eq_rank, _ = prefix(eq_e)
        sel = jnp.maximum(gt_e, jnp.where(eq_rank < need[e], eq_e, 0.0))
        rk, offs = prefix(sel)
        rk = jnp.where(sel > 0.0, rk, -1.0)
        rank_ref[e] = rk.astype(jnp.int32)
        taken = jnp.sum(jnp.sum(sel, axis=0, keepdims=True), axis=1, keepdims=True)
        offs_all = jnp.where(cc == e, offs, offs_all)
        offs_all = jnp.where(cc == n_exp + e, taken, offs_all)
    offs_ref[...] = offs_all.astype(jnp.int32)


def _route(aff_t, cap):
    n_exp, n = aff_t.shape
    assert n % LANES == 0 and n <= ROUTE_BLOCKS * LANES and cap <= n
    a = aff_t.reshape(n_exp, n // LANES, LANES)
    if n < ROUTE_BLOCKS * LANES:
        a = jnp.pad(a, ((0, 0), (0, ROUTE_BLOCKS - n // LANES), (0, 0)), constant_values=-1.0)
    blk = pl.BlockSpec((n_exp, ROUTE_BLOCKS, LANES), lambda i: (0, 0, 0))
    rank, offs = pl.pallas_call(
        functools.partial(_route_kernel, cap=cap, iters=ROUTE_ITERS),
        grid=(1,),
        in_specs=[blk],
        out_specs=[blk, pl.BlockSpec((ROUTE_BLOCKS, LANES), lambda i: (0, 0))],
        out_shape=[jax.ShapeDtypeStruct((n_exp, ROUTE_BLOCKS, LANES), jnp.int32),
                   jax.ShapeDtypeStruct((ROUTE_BLOCKS, LANES), jnp.int32)],
        compiler_params=_cparams(("arbitrary",)),
        name="route",
    )(a)
    offs_flat = jnp.concatenate([offs[:, :n_exp].T.reshape(-1), offs[0, n_exp:2 * n_exp]])
    return a, rank, offs_flat


def _gather_kernel(offs_ref, h_ref, rank_ref, gate_ref, xs_ref, gs_ref, idx_ref, *, cap, spb):
    e = pl.program_id(0)
    sb = pl.program_id(1)
    outs = (xs_ref, gs_ref, idx_ref)

    taken = offs_ref[pl.num_programs(0) * ROUTE_BLOCKS + e]

    @pl.when((sb == 0) & (taken == cap))
    def _():
        for ref in outs:
            ref[0, 0:GATHER_ALIGN, :] = jnp.zeros((GATHER_ALIGN, ref.shape[2]), ref.dtype)
            ref[0, cap:cap + GATHER_WIN, :] = jnp.zeros((GATHER_WIN, ref.shape[2]), ref.dtype)

    @pl.when((sb == 0) & (taken < cap))
    def _():
        for ref in outs:
            ref[...] = jnp.zeros(ref.shape, ref.dtype)

    lane = _iota((1, LANES), 1)

    def place(k, b, r0a, win):
        rows = _iota((win, LANES), 0)
        rrow = rank_ref[0, pl.ds(b, 1), :]
        grow = gate_ref[0, pl.ds(b, 1), :]
        trow = (b * LANES + lane).astype(F32)
        hit = (rrow - r0a) == rows
        hblk = h_ref[pl.ds(pl.multiple_of(k * LANES, LANES), LANES), :]
        onehot = jnp.where(hit, 1.0, 0.0).astype(BF16)
        rows_x = jnp.dot(onehot, hblk, preferred_element_type=F32).astype(BF16)
        rows_g = jnp.sum(jnp.where(hit, grow, 0.0), axis=1, keepdims=True)
        rows_i = jnp.sum(jnp.where(hit, trow, 0.0), axis=1, keepdims=True)
        head = pl.ds(r0a, GATHER_ALIGN)
        tail = pl.ds(r0a + GATHER_ALIGN, win - GATHER_ALIGN)
        for ref, val in zip(outs, (rows_x, rows_g, rows_i)):
            ref[0, head, :] += val[0:GATHER_ALIGN]
            ref[0, tail, :] = val[GATHER_ALIGN:win]

    group = min(spb, GATHER_GROUP)

    def body(i, carry):
        ks = [i * group + u for u in range(group)]
        bs = [sb * spb + k for k in ks]
        r0as, short = [], None
        for b in bs:
            r0 = offs_ref[e * ROUTE_BLOCKS + b]
            r0a = pl.multiple_of((r0 >> 4) << 4, GATHER_ALIGN)
            nxt = offs_ref[jnp.where(b == ROUTE_BLOCKS - 1, pl.num_programs(0) * ROUTE_BLOCKS + e,
                                     e * ROUTE_BLOCKS + b + 1)]
            fits = nxt - r0a <= GATHER_WIN_SHORT - GATHER_ALIGN
            short = fits if short is None else short & fits
            r0as.append(r0a)

        def run(win):
            for k, b, r0a in zip(ks, bs, r0as):
                place(k, b, r0a, win)

        lax.cond(short, lambda: run(GATHER_WIN_SHORT), lambda: run(GATHER_WIN))
        return carry

    lax.fori_loop(0, spb // group, body, 0)


def _gather(offs_flat, h2, rank, gate, cap):
    n, d = h2.shape
    n_exp = rank.shape[0]
    sbt = min(n, GATHER_TOKENS)
    spb = sbt // LANES
    cap_p = cap + GATHER_WIN
    blk = pl.BlockSpec((1, ROUTE_BLOCKS, LANES), lambda e, s, offs: (e, 0, 0))
    out = lambda w: pl.BlockSpec((1, cap_p, w), lambda e, s, offs: (e, 0, 0))
    return pl.pallas_call(
        functools.partial(_gather_kernel, cap=cap, spb=spb),
        grid_spec=pltpu.PrefetchScalarGridSpec(
            num_scalar_prefetch=1,
            grid=(n_exp, n // sbt),
            in_specs=[pl.BlockSpec((sbt, d), lambda e, s, offs: (s, 0)), blk, blk],
            out_specs=[out(d), out(1), out(1)]),
        out_shape=[jax.ShapeDtypeStruct((n_exp, cap_p, d), BF16),
                   jax.ShapeDtypeStruct((n_exp, cap_p, 1), F32),
                   jax.ShapeDtypeStruct((n_exp, cap_p, 1), F32)],
        compiler_params=_cparams(("arbitrary", "arbitrary")),
        name="moe_gather",
    )(offs_flat, h2, rank, gate)


def _ffn_tiles(caps, tm):
    tiles = [[(0, a, min(tm, caps[0] - a))] for a in range(0, caps[0], tm)]
    for s in range(1, len(caps)):
        if caps[s] <= FFN_RIDER_ROWS:
            tiles[-1].append((s, 0, caps[s]))
        else:
            tiles += [[(s, a, min(tm, caps[s] - a))] for a in range(0, caps[s], tm)]
    return tiles


def _ffn_kernel(*refs, caps, tm):
    ns = len(caps)
    xs_refs, (w1_ref, w3_ref, w2_ref) = refs[:ns], refs[ns:ns + 3]
    gs_refs, o_refs = refs[ns + 3:2 * ns + 3], refs[2 * ns + 3:]
    f = pl.program_id(1)

    @pl.when(f == 0)
    def _():
        for o_ref in o_refs:
            o_ref[...] = jnp.zeros(o_ref.shape, F32)

    w1 = w1_ref[0, 0].astype(BF16)
    w3 = w3_ref[0, 0].astype(BF16)
    w2 = w2_ref[0, 0].astype(BF16)
    for tile in _ffn_tiles(caps, tm):
        parts = [xs_refs[s][0, a:a + n, :] for s, a, n in tile]
        x = parts[0] if len(parts) == 1 else jnp.concatenate(parts, axis=0)
        a_ = jnp.dot(x, w1, preferred_element_type=F32)
        b_ = jnp.dot(x, w3, preferred_element_type=F32)
        hid = (a_ * _sigmoid(a_) * b_).astype(BF16)
        y = jnp.dot(hid, w2, preferred_element_type=F32)
        r = 0
        for s, a, n in tile:
            o_refs[s][0, a:a + n, :] += y[r:r + n]
            r += n

    @pl.when(f == pl.num_programs(1) - 1)
    def _():
        for o_ref, gs_ref in zip(o_refs, gs_refs):
            o_ref[0] = o_ref[0] * gs_ref[0]


def _ffn(row_sets, w1, w3, w2, layer):
    caps = tuple(cap for _, _, cap in row_sets)
    n_exp, _, d = row_sets[0][0].shape
    ff = w1.shape[3]
    tf = FFN_TF
    tm = min(caps[0], FFN_TM)
    rows = lambda cap, w: pl.BlockSpec((1, cap, w), lambda e, f: (e, 0, 0))
    outs = pl.pallas_call(
        functools.partial(_ffn_kernel, caps=caps, tm=tm),
        grid=(n_exp, ff // tf),
        in_specs=[rows(c, d) for c in caps]
        + [pl.BlockSpec((1, 1, d, tf), lambda e, f: (layer, e, 0, f)),
           pl.BlockSpec((1, 1, d, tf), lambda e, f: (layer, e, 0, f)),
           pl.BlockSpec((1, 1, tf, d), lambda e, f: (layer, e, f, 0))]
        + [rows(c, 1) for c in caps],
        out_specs=[rows(c, d) for c in caps],
        out_shape=[jax.ShapeDtypeStruct((n_exp, c, d), F32) for c in caps],
        compiler_params=_cparams(("arbitrary", "arbitrary")),
        name="moe_ffn",
    )(*[xs for xs, _, _ in row_sets], w1, w3, w2, *[gs for _, gs, _ in row_sets])
    return outs


def _combine_kernel(offs_ref, idx_ref, out_ref, y_ref, *, cap, tht):
    th = pl.program_id(0)
    e = pl.program_id(1)

    @pl.when(e == 0)
    def _():
        y_ref[...] = jnp.zeros(y_ref.shape, F32)

    bpt = tht // LANES
    first = e * ROUTE_BLOCKS + th * bpt
    start = offs_ref[first]
    is_last = th == pl.num_programs(0) - 1
    taken = offs_ref[pl.num_programs(1) * ROUTE_BLOCKS + e]
    end = jnp.where(is_last, taken, offs_ref[jnp.minimum(first + bpt, (e + 1) * ROUTE_BLOCKS - 1)])
    base = th * tht

    def token(r):
        return jnp.clip(idx_ref[e * cap + r] - base, 0, tht - 1)

    def single(r, carry):
        y_ref[pl.ds(token(r), 1), :] += out_ref[0, pl.ds(r, 1), :]
        return carry

    def tile(i, carry):
        r = pl.multiple_of(mid + SUBLANES * i, SUBLANES)
        rows = out_ref[0, pl.ds(r, SUBLANES), :]
        for g in range(0, SUBLANES, COMBINE_GROUP):
            ts = [token(r + g + u) for u in range(COMBINE_GROUP)]
            sums = [y_ref[pl.ds(ts[u], 1), :] + rows[g + u:g + u + 1, :] for u in range(COMBINE_GROUP)]
            for u in range(COMBINE_GROUP):
                y_ref[pl.ds(ts[u], 1), :] = sums[u]
        return carry

    mid = jnp.minimum(((start + SUBLANES - 1) >> 3) << 3, end)
    tail = jnp.maximum((end >> 3) << 3, mid)
    lax.fori_loop(start, mid, single, 0)
    lax.fori_loop(0, (tail - mid) >> 3, tile, 0)
    lax.fori_loop(tail, end, single, 0)


def _combine(offs_flat, idx_flat, out, n):
    n_exp, cap, d = out.shape
    tht = min(n, COMBINE_TOKENS)
    return pl.pallas_call(
        functools.partial(_combine_kernel, cap=cap, tht=tht),
        grid_spec=pltpu.PrefetchScalarGridSpec(
            num_scalar_prefetch=2,
            grid=(n // tht, n_exp),
            in_specs=[pl.BlockSpec((1, cap, d), lambda t, e, offs, idx: (e, 0, 0))],
            out_specs=pl.BlockSpec((tht, d), lambda t, e, offs, idx: (t, 0))),
        out_shape=jax.ShapeDtypeStruct((n, d), F32),
        compiler_params=_cparams(("arbitrary", "arbitrary")),
        name="moe_combine",
    )(offs_flat, idx_flat, out)


def _moe(routed_sets, w1, w3, w2, layer):
    plans = []
    for h2, aff_t in routed_sets:
        n = h2.shape[0]
        cap = CAPACITY_FACTOR * n // aff_t.shape[0]
        gate, rank, offs_flat = _route(aff_t, cap)
        xs, gs, idx = _gather(offs_flat, h2, rank, gate, cap)
        plans.append((offs_flat, idx[:, :cap, 0].astype(jnp.int32).reshape(-1), xs, gs, cap, n))
    outs = _ffn([(p[2], p[3], p[4]) for p in plans], w1, w3, w2, layer)
    return [_combine(p[0], p[1], out, p[5]) for p, out in zip(plans, outs)]


def _final_kernel(x_ref, y_ref, g2_ref, ng_ref, o_ref):
    x = x_ref[...] + g2_ref[...] * y_ref[...]
    ms = jnp.mean(x * x, axis=-1, keepdims=True)
    o_ref[...] = x * lax.rsqrt(ms + EPS) * ng_ref[...]


def _final(x, y, g2, norm_g):
    n, d = x.shape
    tn = min(n, ROW_TILE)
    row = pl.BlockSpec((tn, d), lambda i: (i, 0))
    vec = pl.BlockSpec((1, d), lambda i: (0, 0))
    return pl.pallas_call(
        _final_kernel,
        grid=(n // tn,),
        in_specs=[row, row, vec, vec],
        out_specs=row,
        out_shape=jax.ShapeDtypeStruct((n, d), F32),
        compiler_params=_cparams(("arbitrary",)),
        name="final_norm",
    )(x, y, g2, norm_g)


def _rope_tables(n):
    pos = jnp.arange(n)
    row, colp = pos // GRID_W, pos % GRID_W
    axis_dim = HEAD_DIM // 2
    inv_freq = ROPE_THETA ** (-jnp.arange(0, axis_dim, 2, dtype=F32) / axis_dim)

    def axis_angles(p):
        a = p.astype(F32)[:, None] * inv_freq[None, :]
        return jnp.concatenate([a, a], axis=-1)

    ang = jnp.concatenate([axis_angles(row), axis_angles(colp)], axis=-1)
    ang = jnp.concatenate([ang, ang], axis=-1)
    return jnp.cos(ang), jnp.sin(ang)


def _zero_state():
    return (jnp.zeros((GROUP_W, GROUP_W), F32), jnp.zeros((SUBLANES, GROUP_W), F32),
            jnp.zeros((SUBLANES, LANES), F32))


def kernel(x, c, ctx, c_ctx, ada_w, ada_b, norm1_g, w_in, conv_w, cmlp_norm_g, cmlp_ws, cmlp_bs, q_norm_g, k_norm_g, ml_igate_b, ml_fgate_b, ml_norm_g, w_out, norm2_g, router_w, exp_w1, exp_w3, exp_w2, final_norm_g):
    assert x.shape[0] == 1 and ctx.shape[0] == 1
    depth, d, d_in = w_in.shape
    n_exp = router_w.shape[-1]
    assert d_in <= N_IN_BLOCKS * LANES and d == 4 * GROUP_W
    xs = x[0]
    xc = ctx[0]
    n = xs.shape[0]

    mod = _modulation(jnp.stack([c[0], c_ctx], axis=1), ada_w, ada_b)
    tables = _rope_tables(n)
    row = lambda v: v.reshape(1, -1)

    res_x = None
    res_c = None
    for l in range(depth):
        parts = [[mod[l, s:s + 1, j * d:(j + 1) * d] for j in range(6)] for s in range(2)]
        sh1, sc1, g1, sh2, sc2, g2 = parts[0]
        csh1, csc1, cg1, csh2, csc2, cg2 = parts[1]
        w_in_l = jnp.pad(w_in[l], ((0, 0), (0, N_IN_BLOCKS * LANES - d_in))).astype(BF16)
        w_out_l = w_out[l].astype(BF16)
        cmlp_ws_l = cmlp_ws[l].astype(BF16)
        cmlp_bias = jnp.repeat(cmlp_bs[l].T, HEAD_DIM, axis=1)
        qg = jnp.tile(q_norm_g[l], 2).reshape(1, LANES)
        kg = jnp.tile(k_norm_g[l], 2).reshape(1, LANES)
        gate_b = jnp.pad(jnp.concatenate([ml_igate_b[l], ml_fgate_b[l]]), (0, LANES - 16)).reshape(1, LANES)
        router_pad = jnp.pad(router_w[l], ((0, 0), (0, LANES - n_exp)))
        update_ctx = l < depth - 1

        zc, xc = _inproj(xc, res_c, row(norm1_g[l]), csh1, csc1, w_in_l)
        n_ctx = xc.shape[0]
        qc_, ktc, vdc = _attprep(zc, qg, kg, None)
        hcf, hcb, st_f, st_b = _mlstm(zc, gate_b, _zero_state(), _zero_state())
        if update_ctx:
            conv_c, cmlp_c = _local_mixers(zc, conv_w[l], row(cmlp_norm_g[l]), cmlp_ws_l, cmlp_bias)
            att_c = _flash(qc_, ktc, vdc, n_ctx)
            xc, hc2, affc = _outproj((conv_c, cmlp_c, att_c, hcf, hcb), zc, row(ml_norm_g[l]), w_out_l, xc, cg1,
                                     row(norm2_g[l]), csh2, csc2, router_pad, n_exp)

        z, xs = _inproj(xs, res_x, row(norm1_g[l]), sh1, sc1, w_in_l)
        conv_x, cmlp_x = _local_mixers(z, conv_w[l], row(cmlp_norm_g[l]), cmlp_ws_l, cmlp_bias)
        q_, kt, vd = _attprep(z, qg, kg, tables)
        att_x = _flash(q_, jnp.concatenate([ktc, kt], axis=1), jnp.concatenate([vdc, vd], axis=0), n_ctx + n)
        hf, hb, _, _ = _mlstm(z, gate_b, st_f, st_b)
        xs, h2, aff = _outproj((conv_x, cmlp_x, att_x, hf, hb), z, row(ml_norm_g[l]), w_out_l, xs, g1,
                               row(norm2_g[l]), sh2, sc2, router_pad, n_exp)
        routed = [(h2, aff)] + ([(hc2, affc)] if update_ctx else [])
        ys = _moe(routed, exp_w1, exp_w3, exp_w2, l)
        res_x = (ys[0], g2)
        if update_ctx:
            res_c = (ys[1], cg2)

    return _final(xs, res_x[0], res_x[1], row(final_norm_g))[None]
```
